```python
import math
import jax, jax.numpy as jnp
from jax import lax
import numpy as np

D_MODEL = 2048
BATCH = 4
SEQ = 2048
DEPTH = 4
DEC_BATCH = 128
DEC_SEQ = 4
PAST_LEN = 8192
PAGE_SIZE = 128

N_MIXERS = 3
N_MLA_LAYERS = (DEPTH + 2) // 3
N_GDN_LAYERS = (DEPTH + 1) // 3
N_DSA_LAYERS = DEPTH // 3

ROPE_THETA = 500000.0
LN_EPS = 1e-5
RMS_EPS = 1e-6
DEEPNORM_ALPHA = (2 * DEPTH) ** 0.25
DEEPNORM_BETA = (8 * DEPTH) ** -0.25
D_FF = 4 * D_MODEL
Q_BLOCK = 128

MLA_HEADS = 16
MLA_Q_LORA = 512
MLA_KV_LORA = 256
MLA_NOPE = 128
MLA_ROPE = 64
MLA_V = 128
MLA_IN = MLA_Q_LORA + MLA_KV_LORA + MLA_ROPE
MLA_CACHE = MLA_KV_LORA + MLA_ROPE
MLA_SCALE = (MLA_NOPE + MLA_ROPE) ** -0.5

GDN_HK = 16
GDN_HV = 32
GDN_DK = 128
GDN_DV = 128
GDN_CONV = 4
GDN_CHUNK = 64
GDN_QK = GDN_HK * GDN_DK
GDN_VD = GDN_HV * GDN_DV
GDN_CONV_DIM = 2 * GDN_QK + GDN_VD
GDN_IN = GDN_CONV_DIM + GDN_VD + 2 * GDN_HV

DSA_HEADS = 16
DSA_KV_HEADS = 4
DSA_HD = 128
DSA_ROT = DSA_HD // 4
DSA_IDX_HEADS = 16
DSA_IDX_DIM = 128
DSA_IDX_ROT = DSA_IDX_DIM // 4
DSA_TOPK_MAX = 256
DSA_OFF_K = DSA_HEADS * DSA_HD
DSA_OFF_V = DSA_OFF_K + DSA_KV_HEADS * DSA_HD
DSA_OFF_IQ = DSA_OFF_V + DSA_KV_HEADS * DSA_HD
DSA_OFF_IK = DSA_OFF_IQ + DSA_IDX_HEADS * DSA_IDX_DIM
DSA_OFF_IW = DSA_OFF_IK + DSA_IDX_DIM
DSA_IN = DSA_OFF_IW + DSA_IDX_HEADS

kernel_name = 'hybrid_mla_gdn_dsa_decoder_step'


def layer_norm(x, g, b):
    xf = x.astype(jnp.float32)
    mu = jnp.mean(xf, -1, keepdims=True)
    var = jnp.mean(jnp.square(xf - mu), -1, keepdims=True)
    return ((xf - mu) * lax.rsqrt(var + LN_EPS) * g + b).astype(x.dtype)


def rms_norm(x, g):
    xf = x.astype(jnp.float32)
    return (xf * lax.rsqrt(jnp.mean(jnp.square(xf), -1, keepdims=True) + RMS_EPS) * g).astype(x.dtype)


def l2_norm(x):
    xf = x.astype(jnp.float32)
    return xf * lax.rsqrt(jnp.sum(jnp.square(xf), -1, keepdims=True) + RMS_EPS)


def rope(x, pos, rot_dim):
    half = rot_dim // 2
    inv = jnp.power(ROPE_THETA, -jnp.arange(half, dtype=jnp.float32) * 2.0 / rot_dim)
    ang = pos.astype(jnp.float32)[:, None] * inv[None, :]
    cos = jnp.cos(ang)[:, None, :]
    sin = jnp.sin(ang)[:, None, :]
    xf = x.astype(jnp.float32)
    x1 = xf[..., :half]
    x2 = xf[..., half:rot_dim]
    out = jnp.concatenate([x1 * cos - x2 * sin, x2 * cos + x1 * sin, xf[..., rot_dim:]], -1)
    return out.astype(x.dtype)


def _blocks(t):
    b, t_len = t.shape[:2]
    return jnp.moveaxis(t.reshape((b, t_len // Q_BLOCK, Q_BLOCK) + t.shape[2:]), 1, 0)


def _unblocks(t):
    nb, b, qb = t.shape[:3]
    return jnp.moveaxis(t, 0, 1).reshape((b, nb * qb) + t.shape[3:])


def _gather_rows(src, idx):
    return jax.vmap(lambda s, i: s[i])(src, idx)


def sq_relu_mlp(x, w_up, w_down):
    return jnp.square(jax.nn.relu(x @ w_up)) @ w_down


def mla_project(x, pos, w_in, g_q, g_kv, w_uq, w_uk):
    b, t, _ = x.shape
    h = x @ w_in
    c_q = rms_norm(h[..., :MLA_Q_LORA], g_q)
    c_kv = rms_norm(h[..., MLA_Q_LORA:MLA_Q_LORA + MLA_KV_LORA], g_kv)
    k_rope = rope(h[..., MLA_Q_LORA + MLA_KV_LORA:][:, :, None, :], pos, MLA_ROPE)[:, :, 0, :]
    q = (c_q @ w_uq).reshape(b, t, MLA_HEADS, MLA_NOPE + MLA_ROPE)
    q_rope = rope(q[..., MLA_NOPE:], pos, MLA_ROPE)
    q_lat = jnp.einsum('bthn,chn->bthc', q[..., :MLA_NOPE], w_uk)
    q_cat = jnp.concatenate([q_lat, q_rope], -1)
    rows = jnp.concatenate([c_kv, k_rope], -1)
    return q_cat, rows


def mla_attend_prompt(q_cat, rows):
    t_len = rows.shape[1]
    key_pos = jnp.arange(t_len)
    c = rows[..., :MLA_KV_LORA]

    def block(args):
        q_b, i = args
        qpos = i * Q_BLOCK + jnp.arange(Q_BLOCK)
        s = jnp.einsum('bqhc,bsc->bhqs', q_b, rows).astype(jnp.float32) * MLA_SCALE
        s = jnp.where(key_pos[None, :] <= qpos[:, None], s, -jnp.inf)
        p = jax.nn.softmax(s, axis=-1).astype(rows.dtype)
        return jnp.einsum('bhqs,bsc->bqhc', p, c)

    o = lax.map(block, (_blocks(q_cat), jnp.arange(t_len // Q_BLOCK)))
    return _unblocks(o)


def mla_attend_sample(q_cat, rows_new, cache, page_table):
    db, t_len = rows_new.shape[:2]
    past = cache[page_table].reshape(db, -1, MLA_CACHE)
    n_past = past.shape[1]
    s_past = jnp.einsum('bqhc,bsc->bhqs', q_cat, past).astype(jnp.float32)
    s_new = jnp.einsum('bqhc,bsc->bhqs', q_cat, rows_new).astype(jnp.float32)
    tri = jnp.arange(t_len)[None, :] <= jnp.arange(t_len)[:, None]
    s_new = jnp.where(tri, s_new, -jnp.inf)
    p = jax.nn.softmax(jnp.concatenate([s_past, s_new], -1) * MLA_SCALE, axis=-1).astype(rows_new.dtype)
    return (jnp.einsum('bhqs,bsc->bqhc', p[..., :n_past], past[..., :MLA_KV_LORA])
            + jnp.einsum('bhqs,bsc->bqhc', p[..., n_past:], rows_new[..., :MLA_KV_LORA]))


def mla_out(o_lat, w_uv, w_out):
    b, t = o_lat.shape[:2]
    v = jnp.einsum('bthc,chv->bthv', o_lat, w_uv)
    return v.reshape(b, t, MLA_HEADS * MLA_V) @ w_out


def gated_delta_chunked(q, k, v, g, beta, s0, chunk):
    b, t_len, h, dk = q.shape
    dv = v.shape[-1]
    n = t_len // chunk

    def c4(x):
        return x.astype(jnp.float32).reshape(b, n, chunk, h, x.shape[-1]).transpose(1, 0, 3, 2, 4)

    def c3(x):
        return x.astype(jnp.float32).reshape(b, n, chunk, h).transpose(1, 0, 3, 2)

    qc, kc, vc = c4(q), c4(k), c4(v)
    gcum = jnp.cumsum(c3(g), -1)
    bc = c3(beta)
    idx = jnp.arange(chunk)
    lower_incl = idx[:, None] >= idx[None, :]
    strict = idx[:, None] > idx[None, :]
    diff = gcum[..., :, None] - gcum[..., None, :]
    decay = jnp.where(lower_incl, jnp.exp(jnp.where(lower_incl, diff, 0.0)), 0.0)
    kb = kc * bc[..., None]
    vb = vc * bc[..., None]
    l_mat = jnp.where(strict, jnp.einsum('nbhid,nbhjd->nbhij', kb, kc) * decay, 0.0)
    rhs = jnp.concatenate([vb, kb * jnp.exp(gcum)[..., None]], -1)
    sol = lax.linalg.triangular_solve(l_mat, rhs, left_side=True, lower=True, unit_diagonal=True)
    u = sol[..., :dv]
    w = sol[..., dv:]
    a_intra = jnp.where(lower_incl, jnp.einsum('nbhid,nbhjd->nbhij', qc, kc) * decay, 0.0)

    def step(s, inp):
        q_i, k_i, u_i, w_i, g_i, a_i = inp
        v_new = u_i - jnp.einsum('bhck,bhkv->bhcv', w_i, s)
        o = (jnp.einsum('bhck,bhkv->bhcv', q_i * jnp.exp(g_i)[..., None], s)
             + jnp.einsum('bhij,bhjv->bhiv', a_i, v_new))
        g_last = g_i[..., -1]
        s = (s * jnp.exp(g_last)[..., None, None]
             + jnp.einsum('bhck,bhcv->bhkv', k_i * jnp.exp(g_last[..., None] - g_i)[..., None], v_new))
        return s, o

    s_fin, o = lax.scan(step, s0.astype(jnp.float32), (qc, kc, u, w, gcum, a_intra))
    o = o.transpose(1, 0, 3, 2, 4).reshape(b, t_len, h, dv)
    return o, s_fin


def gdn_mixer(x, conv_buf, s0, w_in, w_conv, a_log, dt_bias, g_out, w_out):
    b, t_len, _ = x.shape
    h = x @ w_in
    mixed = h[..., :GDN_CONV_DIM]
    z = h[..., GDN_CONV_DIM:GDN_CONV_DIM + GDN_VD].reshape(b, t_len, GDN_HV, GDN_DV)
    beta_in = h[..., GDN_CONV_DIM + GDN_VD:GDN_CONV_DIM + GDN_VD + GDN_HV]
    a_in = h[..., GDN_CONV_DIM + GDN_VD + GDN_HV:]
    xp = jnp.concatenate([conv_buf.astype(mixed.dtype), mixed], 1)
    conv = xp[:, 0:t_len] * w_conv[:, 0]
    for j in range(1, GDN_CONV):
        conv = conv + xp[:, j:j + t_len] * w_conv[:, j]
    conv = jax.nn.silu(conv)
    new_buf = xp[:, t_len:]
    rep = GDN_HV // GDN_HK
    q = conv[..., :GDN_QK].reshape(b, t_len, GDN_HK, GDN_DK)
    k = conv[..., GDN_QK:2 * GDN_QK].reshape(b, t_len, GDN_HK, GDN_DK)
    v = conv[..., 2 * GDN_QK:].reshape(b, t_len, GDN_HV, GDN_DV)
    q = l2_norm(jnp.repeat(q, rep, axis=2)) * (GDN_DK ** -0.5)
    k = l2_norm(jnp.repeat(k, rep, axis=2))
    beta = jax.nn.sigmoid(beta_in.astype(jnp.float32))
    g = -jnp.exp(a_log.astype(jnp.float32)) * jax.nn.softplus(a_in.astype(jnp.float32) + dt_bias.astype(jnp.float32))
    chunk = GDN_CHUNK if t_len % GDN_CHUNK == 0 else t_len
    o, s_fin = gated_delta_chunked(q, k, v, g, beta, s0, chunk)
    o = rms_norm(o.astype(x.dtype), g_out) * jax.nn.silu(z)
    return o.reshape(b, t_len, GDN_VD) @ w_out, new_buf, s_fin.astype(s0.dtype)


def dsa_project(x, pos, w_in):
    b, t, _ = x.shape
    h = x @ w_in
    q = rope(h[..., :DSA_OFF_K].reshape(b, t, DSA_HEADS, DSA_HD), pos, DSA_ROT)
    k = rope(h[..., DSA_OFF_K:DSA_OFF_V].reshape(b, t, DSA_KV_HEADS, DSA_HD), pos, DSA_ROT)
    v = h[..., DSA_OFF_V:DSA_OFF_IQ].reshape(b, t, DSA_KV_HEADS, DSA_HD)
    iq = rope(h[..., DSA_OFF_IQ:DSA_OFF_IK].reshape(b, t, DSA_IDX_HEADS, DSA_IDX_DIM), pos, DSA_IDX_ROT)
    ik = rope(h[..., DSA_OFF_IK:DSA_OFF_IW][:, :, None, :], pos, DSA_IDX_ROT)[:, :, 0, :]
    iw = h[..., DSA_OFF_IW:] * ((DSA_IDX_HEADS ** -0.5) * (DSA_IDX_DIM ** -0.5))
    kv = jnp.stack([k, v], 2)
    return q, kv, iq, ik, iw


def index_scores(iq, iw, ik):
    s = jax.nn.relu(jnp.einsum('bqhi,bsi->bqhs', iq, ik).astype(jnp.float32))
    return jnp.einsum('bqh,bqhs->bqs', iw.astype(jnp.float32), s)


def sparse_attend(q, kv_sel, valid):
    b, t, _, _ = q.shape
    qg = q.reshape(b, t, DSA_KV_HEADS, DSA_HEADS // DSA_KV_HEADS, DSA_HD)
    s = jnp.einsum('btjgd,btnjd->btjgn', qg, kv_sel[:, :, :, 0]).astype(jnp.float32) * (DSA_HD ** -0.5)
    s = jnp.where(valid[:, :, None, None, :], s, -jnp.inf)
    p = jax.nn.softmax(s, axis=-1).astype(q.dtype)
    o = jnp.einsum('btjgn,btnjd->btjgd', p, kv_sel[:, :, :, 1])
    return o.reshape(b, t, DSA_HEADS * DSA_HD)


def dsa_attend_prompt(q, kv, iq, ik, iw):
    t_len = kv.shape[1]
    topk = min(DSA_TOPK_MAX, t_len // 4)
    key_pos = jnp.arange(t_len)

    def block(args):
        q_b, iq_b, iw_b, i = args
        qpos = i * Q_BLOCK + jnp.arange(Q_BLOCK)
        sc = index_scores(iq_b, iw_b, ik)
        sc = jnp.where(key_pos[None, :] <= qpos[:, None], sc, -jnp.inf)
        _, sel = lax.top_k(sc, topk)
        valid = sel <= qpos[None, :, None]
        return sparse_attend(q_b, _gather_rows(kv, sel), valid)

    o = lax.map(block, (_blocks(q), _blocks(iq), _blocks(iw), jnp.arange(t_len // Q_BLOCK)))
    return _unblocks(o)


def dsa_attend_sample(q, kv_new, iq, ik_new, iw, cache_kv, cache_idx, page_table):
    db, t_len = kv_new.shape[:2]
    n_past = page_table.shape[1] * PAGE_SIZE
    topk = min(DSA_TOPK_MAX, (n_past + t_len) // 4)
    ik_past = cache_idx[page_table].reshape(db, n_past, DSA_IDX_DIM)
    sc = jnp.concatenate([index_scores(iq, iw, ik_past), index_scores(iq, iw, ik_new)], -1)
    qpos = n_past + jnp.arange(t_len)
    sc = jnp.where(jnp.arange(n_past + t_len)[None, :] <= qpos[:, None], sc, -jnp.inf)
    _, sel = lax.top_k(sc, topk)
    valid = sel <= qpos[None, :, None]
    ps = jnp.minimum(sel, n_past - 1)
    phys = jnp.take_along_axis(page_table, (ps // PAGE_SIZE).reshape(db, -1), axis=1).reshape(ps.shape)
    kv_past = cache_kv[phys, ps % PAGE_SIZE]
    kv_cur = _gather_rows(kv_new, jnp.clip(sel - n_past, 0, t_len - 1))
    kv_sel = jnp.where((sel < n_past)[..., None, None, None], kv_past.astype(kv_cur.dtype), kv_cur)
    return sparse_attend(q, kv_sel, valid)


def setup_inputs(seed: int = 0) -> dict:
    key = jax.random.key(seed)
    keys = iter(jax.random.split(key, 48))

    def nrm(shape, scale=1.0):
        return jax.random.normal(next(keys), shape, jnp.float32) * scale

    def gain(shape):
        return 1.0 + nrm(shape, 0.02)

    n_pages = PAST_LEN // PAGE_SIZE
    n_used = DEC_BATCH * n_pages
    n_pool = n_used + max(1, n_used // 4)
    page_table = jax.random.permutation(next(keys), n_pool)[:n_used].reshape(DEC_BATCH, n_pages).astype(jnp.int32)
    a_lin = jax.random.uniform(next(keys), (N_GDN_LAYERS, GDN_HV), jnp.float32, 1.0, 16.0)
    dt = jnp.exp(jax.random.uniform(next(keys), (N_GDN_LAYERS, GDN_HV), jnp.float32,
                                    math.log(1e-3), math.log(1e-1)))
    dt_bias = dt + jnp.log(-jnp.expm1(-dt))
    return {
        'x_prompt': nrm((BATCH, SEQ, D_MODEL)),
        'x_sample': nrm((DEC_BATCH, DEC_SEQ, D_MODEL)),
        'cache_mla': nrm((N_MLA_LAYERS, n_pool, PAGE_SIZE, MLA_CACHE)),
        'state_gdn_S': nrm((N_GDN_LAYERS, DEC_BATCH, GDN_HV, GDN_DK, GDN_DV), 0.1),
        'state_gdn_conv': nrm((N_GDN_LAYERS, DEC_BATCH, GDN_CONV - 1, GDN_CONV_DIM)),
        'cache_dsa_kv': nrm((N_DSA_LAYERS, n_pool, PAGE_SIZE, 2, DSA_KV_HEADS, DSA_HD)),
        'cache_dsa_idx': nrm((N_DSA_LAYERS, n_pool, PAGE_SIZE, DSA_IDX_DIM)),
        'page_table': page_table,
        'w_mla_in': nrm((N_MLA_LAYERS, D_MODEL, MLA_IN), D_MODEL ** -0.5),
        'g_mla_q': gain((N_MLA_LAYERS, MLA_Q_LORA)),
        'g_mla_kv': gain((N_MLA_LAYERS, MLA_KV_LORA)),
        'w_mla_uq': nrm((N_MLA_LAYERS, MLA_Q_LORA, MLA_HEADS * (MLA_NOPE + MLA_ROPE)), MLA_Q_LORA ** -0.5),
        'w_mla_uk': nrm((N_MLA_LAYERS, MLA_KV_LORA, MLA_HEADS, MLA_NOPE), MLA_KV_LORA ** -0.5),
        'w_mla_uv': nrm((N_MLA_LAYERS, MLA_KV_LORA, MLA_HEADS, MLA_V), MLA_KV_LORA ** -0.5),
        'w_mla_out': nrm((N_MLA_LAYERS, MLA_HEADS * MLA_V, D_MODEL), DEEPNORM_BETA * (MLA_HEADS * MLA_V) ** -0.5),
        'w_gdn_in': nrm((N_GDN_LAYERS, D_MODEL, GDN_IN), D_MODEL ** -0.5),
        'w_gdn_conv': nrm((N_GDN_LAYERS, GDN_CONV_DIM, GDN_CONV), GDN_CONV ** -0.5),
        'gdn_a_log': jnp.log(a_lin),
        'gdn_dt_bias': dt_bias,
        'g_gdn_out': gain((N_GDN_LAYERS, GDN_DV)),
        'w_gdn_out': nrm((N_GDN_LAYERS, GDN_VD, D_MODEL), DEEPNORM_BETA * GDN_VD ** -0.5),
        'w_dsa_in': nrm((N_DSA_LAYERS, D_MODEL, DSA_IN), D_MODEL ** -0.5),
        'w_dsa_out': nrm((N_DSA_LAYERS, DSA_HEADS * DSA_HD, D_MODEL), DEEPNORM_BETA * (DSA_HEADS * DSA_HD) ** -0.5),
        'ln1_g': gain((DEPTH, D_MODEL)),
        'ln1_b': nrm((DEPTH, D_MODEL), 0.02),
        'ln2_g': gain((DEPTH, D_MODEL)),
        'ln2_b': nrm((DEPTH, D_MODEL), 0.02),
        'w_up': nrm((DEPTH, D_MODEL, D_FF), D_MODEL ** -0.5),
        'w_down': nrm((DEPTH, D_FF, D_MODEL), DEEPNORM_BETA * D_FF ** -0.5),
    }


def reference(x_prompt, x_sample, cache_mla, state_gdn_S, state_gdn_conv, cache_dsa_kv, cache_dsa_idx,
              page_table, w_mla_in, g_mla_q, g_mla_kv, w_mla_uq, w_mla_uk, w_mla_uv, w_mla_out,
              w_gdn_in, w_gdn_conv, gdn_a_log, gdn_dt_bias, g_gdn_out, w_gdn_out, w_dsa_in, w_dsa_out,
              ln1_g, ln1_b, ln2_g, ln2_b, w_up, w_down):
    bp, tp, _ = x_prompt.shape
    pos_p = jnp.arange(tp)
    pos_s = page_table.shape[1] * PAGE_SIZE + jnp.arange(x_sample.shape[1])
    xp, xs = x_prompt, x_sample
    mla_p, mla_s = [], []
    gs_p, gc_p, gs_s, gc_s = [], [], [], []
    dkv_p, dix_p, dkv_s, dix_s = [], [], [], []
    for i in range(DEPTH):
        kind = i % N_MIXERS
        j = i // N_MIXERS
        if kind == 0:
            qc_p, rows_p = mla_project(xp, pos_p, w_mla_in[j], g_mla_q[j], g_mla_kv[j], w_mla_uq[j], w_mla_uk[j])
            mix_p = mla_out(mla_attend_prompt(qc_p, rows_p), w_mla_uv[j], w_mla_out[j])
            qc_s, rows_s = mla_project(xs, pos_s, w_mla_in[j], g_mla_q[j], g_mla_kv[j], w_mla_uq[j], w_mla_uk[j])
            mix_s = mla_out(mla_attend_sample(qc_s, rows_s, cache_mla[j], page_table), w_mla_uv[j], w_mla_out[j])
            mla_p.append(rows_p)
            mla_s.append(rows_s)
        elif kind == 1:
            conv0 = jnp.zeros((bp, GDN_CONV - 1, GDN_CONV_DIM), xp.dtype)
            s0 = jnp.zeros((bp, GDN_HV, GDN_DK, GDN_DV), state_gdn_S.dtype)
            mix_p, buf_p, sf_p = gdn_mixer(xp, conv0, s0, w_gdn_in[j], w_gdn_conv[j], gdn_a_log[j],
                                           gdn_dt_bias[j], g_gdn_out[j], w_gdn_out[j])
            mix_s, buf_s, sf_s = gdn_mixer(xs, state_gdn_conv[j], state_gdn_S[j], w_gdn_in[j], w_gdn_conv[j],
                                           gdn_a_log[j], gdn_dt_bias[j], g_gdn_out[j], w_gdn_out[j])
            gs_p.append(sf_p)
            gc_p.append(buf_p)
            gs_s.append(sf_s)
            gc_s.append(buf_s)
        else:
            q_p, kv_p, iq_p, ik_p, iw_p = dsa_project(xp, pos_p, w_dsa_in[j])
            mix_p = dsa_attend_prompt(q_p, kv_p, iq_p, ik_p, iw_p) @ w_dsa_out[j]
            q_s, kv_s, iq_s, ik_s, iw_s = dsa_project(xs, pos_s, w_dsa_in[j])
            mix_s = dsa_attend_sample(q_s, kv_s, iq_s, ik_s, iw_s, cache_dsa_kv[j], cache_dsa_idx[j],
                                      page_table) @ w_dsa_out[j]
            dkv_p.append(kv_p)
            dix_p.append(ik_p)
            dkv_s.append(kv_s)
            dix_s.append(ik_s)
        xp = layer_norm(DEEPNORM_ALPHA * xp + mix_p, ln1_g[i], ln1_b[i])
        xs = layer_norm(DEEPNORM_ALPHA * xs + mix_s, ln1_g[i], ln1_b[i])
        xp = layer_norm(DEEPNORM_ALPHA * xp + sq_relu_mlp(xp, w_up[i], w_down[i]), ln2_g[i], ln2_b[i])
        xs = layer_norm(DEEPNORM_ALPHA * xs + sq_relu_mlp(xs, w_up[i], w_down[i]), ln2_g[i], ln2_b[i])
    return (xp, xs, jnp.stack(mla_p), jnp.stack(mla_s), jnp.stack(gs_p), jnp.stack(gc_p),
            jnp.stack(gs_s), jnp.stack(gc_s), jnp.stack(dkv_p), jnp.stack(dix_p),
            jnp.stack(dkv_s), jnp.stack(dix_s))
```

```python
import functools
import math

import jax
import jax.numpy as jnp
from jax import lax
from jax.experimental import pallas as pl
from jax.experimental.pallas import tpu as pltpu

f32 = jnp.float32
bf16 = jnp.bfloat16

D_MODEL = 2048
BATCH = 4
SEQ = 2048
DEPTH = 4
DEC_BATCH = 128
DEC_SEQ = 4
PAST_LEN = 8192
PAGE_SIZE = 128
N_PAGES = PAST_LEN // PAGE_SIZE
N_MIXERS = 3

ROPE_THETA = 500000.0
LN_EPS = 1e-5
RMS_EPS = 1e-6
DEEPNORM_ALPHA = (2 * DEPTH) ** 0.25
D_FF = 4 * D_MODEL

MLA_HEADS = 16
MLA_Q_LORA = 512
MLA_KV_LORA = 256
MLA_NOPE = 128
MLA_ROPE = 64
MLA_V = 128
MLA_CACHE = MLA_KV_LORA + MLA_ROPE
MLA_SCALE = (MLA_NOPE + MLA_ROPE) ** -0.5

GDN_HK = 16
GDN_HV = 32
GDN_DK = 128
GDN_DV = 128
GDN_CONV = 4
GDN_CHUNK = 64
GDN_QK = GDN_HK * GDN_DK
GDN_VD = GDN_HV * GDN_DV
GDN_CONV_DIM = 2 * GDN_QK + GDN_VD

DSA_HEADS = 16
DSA_KV_HEADS = 4
DSA_HD = 128
DSA_ROT = DSA_HD // 4
DSA_IDX_HEADS = 16
DSA_IDX_DIM = 128
DSA_TOPK_MAX = 256
DSA_SCALE = DSA_HD ** -0.5
DSA_IW_SCALE = (DSA_IDX_HEADS ** -0.5) * (DSA_IDX_DIM ** -0.5)

NP_TOK = BATCH * SEQ
NS_TOK = DEC_BATCH * DEC_SEQ
NTOK = NP_TOK + NS_TOK

LANES = 128
VMEM_BIG = 56 << 20
NEG_BIG = -1e30
INT_MIN = -(2 ** 31)


def _params(sem, vmem=None):
    return pltpu.CompilerParams(dimension_semantics=sem, vmem_limit_bytes=vmem)


def _nt(a, b):
    return lax.dot_general(a, b, (((1,), (1,)), ((), ())), preferred_element_type=f32)


def _tn(a, b):
    return lax.dot_general(a, b, (((0,), (0,)), ((), ())), preferred_element_type=f32)


def _dot(a, b):
    return jnp.dot(a, b, preferred_element_type=f32)


def _dot_hi(a, b):
    return jnp.dot(a, b, preferred_element_type=f32, precision=lax.Precision.HIGHEST)


def _rope_tile(x, c, sa, sb, half):
    return x * c + pltpu.roll(x, LANES - half, 1) * sa + pltpu.roll(x, half, 1) * sb


def _layer_norm(z, g, b):
    mu = jnp.mean(z, -1, keepdims=True)
    zc = z - mu
    var = jnp.mean(zc * zc, -1, keepdims=True)
    return zc * lax.rsqrt(var + LN_EPS) * g + b


def _rope_tables(pos, rot_dim):
    half = rot_dim // 2
    inv = jnp.power(ROPE_THETA, -jnp.arange(half, dtype=f32) * 2.0 / rot_dim)
    ang = pos.astype(f32)[:, None] * inv[None, :]
    cos, sin = jnp.cos(ang), jnp.sin(ang)
    n = pos.shape[0]
    pad = LANES - rot_dim
    c = jnp.concatenate([cos, cos, jnp.ones((n, pad), f32)], 1)
    sa = jnp.concatenate([-sin, jnp.zeros((n, half + pad), f32)], 1)
    sb = jnp.concatenate([jnp.zeros((n, half), f32), sin, jnp.zeros((n, pad), f32)], 1)
    return c, sa, sb


def _mm_kernel(x_ref, w_ref, o_ref, xb_ref):
    @pl.when(pl.program_id(1) == 0)
    def _():
        xb_ref[...] = x_ref[...].astype(bf16)

    o_ref[...] = _dot(xb_ref[...], w_ref[...]).astype(o_ref.dtype)


def _matmul(x, w, *, tm, tn, out_dtype=f32):
    m, k = x.shape
    n = w.shape[1]
    return pl.pallas_call(
        _mm_kernel,
        grid=(m // tm, n // tn),
        in_specs=[pl.BlockSpec((tm, k), lambda i, j: (i, 0)),
                  pl.BlockSpec((k, tn), lambda i, j: (0, j))],
        out_specs=pl.BlockSpec((tm, tn), lambda i, j: (i, j)),
        out_shape=jax.ShapeDtypeStruct((m, n), out_dtype),
        scratch_shapes=[pltpu.VMEM((tm, k), bf16)],
        compiler_params=_params(("parallel", "arbitrary"), VMEM_BIG),
        name="matmul",
    )(x, w)


def _proj_ln_kernel(a_ref, w_ref, x_ref, g_ref, b_ref, y_ref, acc_ref):
    k = pl.program_id(1)

    @pl.when(k == 0)
    def _():
        acc_ref[...] = jnp.zeros_like(acc_ref)

    acc_ref[...] += _dot(a_ref[...], w_ref[...])

    @pl.when(k == pl.num_programs(1) - 1)
    def _():
        z = DEEPNORM_ALPHA * x_ref[...] + acc_ref[...]
        y_ref[...] = _layer_norm(z, g_ref[...], b_ref[...])


def _proj_ln(a, w, x, g, b, *, tm=512, tk=1024):
    m, kd = a.shape
    d = w.shape[1]
    return pl.pallas_call(
        _proj_ln_kernel,
        grid=(m // tm, kd // tk),
        in_specs=[pl.BlockSpec((tm, tk), lambda i, k: (i, k)),
                  pl.BlockSpec((tk, d), lambda i, k: (k, 0)),
                  pl.BlockSpec((tm, d), lambda i, k: (i, 0)),
                  pl.BlockSpec((1, d), lambda i, k: (0, 0)),
                  pl.BlockSpec((1, d), lambda i, k: (0, 0))],
        out_specs=pl.BlockSpec((tm, d), lambda i, k: (i, 0)),
        out_shape=jax.ShapeDtypeStruct((m, d), f32),
        scratch_shapes=[pltpu.VMEM((tm, d), f32)],
        compiler_params=_params(("parallel", "arbitrary"), VMEM_BIG),
        name="proj_ln",
    )(a, w, x, g.reshape(1, d), b.reshape(1, d))


def _mlp_ln_kernel(x_ref, wu_ref, wd_ref, g_ref, b_ref, y_ref, xb_ref, acc_ref):
    j = pl.program_id(1)

    @pl.when(j == 0)
    def _():
        xb_ref[...] = x_ref[...].astype(bf16)
        acc_ref[...] = jnp.zeros_like(acc_ref)

    h = jnp.maximum(_dot(xb_ref[...], wu_ref[...]), 0.0)
    acc_ref[...] += _dot((h * h).astype(bf16), wd_ref[...])

    @pl.when(j == pl.num_programs(1) - 1)
    def _():
        z = DEEPNORM_ALPHA * x_ref[...] + acc_ref[...]
        y_ref[...] = _layer_norm(z, g_ref[...], b_ref[...])


def _mlp_ln(x, w_up, w_down, g, b, *, tm=512, tf=512):
    m, d = x.shape
    ff = w_up.shape[1]
    return pl.pallas_call(
        _mlp_ln_kernel,
        grid=(m // tm, ff // tf),
        in_specs=[pl.BlockSpec((tm, d), lambda i, j: (i, 0)),
                  pl.BlockSpec((d, tf), lambda i, j: (0, j)),
                  pl.BlockSpec((tf, d), lambda i, j: (j, 0)),
                  pl.BlockSpec((1, d), lambda i, j: (0, 0)),
                  pl.BlockSpec((1, d), lambda i, j: (0, 0))],
        out_specs=pl.BlockSpec((tm, d), lambda i, j: (i, 0)),
        out_shape=jax.ShapeDtypeStruct((m, d), f32),
        scratch_shapes=[pltpu.VMEM((tm, d), bf16), pltpu.VMEM((tm, d), f32)],
        compiler_params=_params(("parallel", "arbitrary"), VMEM_BIG),
        name="mlp_ln",
    )(x, w_up, w_down, g.reshape(1, d), b.reshape(1, d))


MLA_IN_PAD = MLA_Q_LORA + MLA_KV_LORA + LANES


def _mla_in_kernel(x_ref, w_ref, gq_ref, gkv_ref, c_ref, sa_ref, sb_ref, cq_ref, rows_ref):
    h = _dot(x_ref[...].astype(bf16), w_ref[...])
    hq = h[:, :MLA_Q_LORA]
    hkv = h[:, MLA_Q_LORA:MLA_Q_LORA + MLA_KV_LORA]
    hr = h[:, MLA_Q_LORA + MLA_KV_LORA:]
    cq = hq * lax.rsqrt(jnp.mean(hq * hq, -1, keepdims=True) + RMS_EPS) * gq_ref[...]
    ckv = hkv * lax.rsqrt(jnp.mean(hkv * hkv, -1, keepdims=True) + RMS_EPS) * gkv_ref[...]
    kr = _rope_tile(hr, c_ref[...], sa_ref[...], sb_ref[...], MLA_ROPE // 2)
    cq_ref[...] = cq.astype(bf16)
    rows_ref[:, :MLA_KV_LORA] = ckv
    rows_ref[:, MLA_KV_LORA:] = kr[:, :MLA_ROPE]


def _mla_in(x, w_pad, g_q, g_kv, tabs, *, tm=512):
    m, d = x.shape
    tab_spec = pl.BlockSpec((tm, LANES), lambda i: (i, 0))
    return pl.pallas_call(
        _mla_in_kernel,
        grid=(m // tm,),
        in_specs=[pl.BlockSpec((tm, d), lambda i: (i, 0)),
                  pl.BlockSpec((d, MLA_IN_PAD), lambda i: (0, 0)),
                  pl.BlockSpec((1, MLA_Q_LORA), lambda i: (0, 0)),
                  pl.BlockSpec((1, MLA_KV_LORA), lambda i: (0, 0)),
                  tab_spec, tab_spec, tab_spec],
        out_specs=[pl.BlockSpec((tm, MLA_Q_LORA), lambda i: (i, 0)),
                   pl.BlockSpec((tm, MLA_CACHE), lambda i: (i, 0))],
        out_shape=[jax.ShapeDtypeStruct((m, MLA_Q_LORA), bf16),
                   jax.ShapeDtypeStruct((m, MLA_CACHE), f32)],
        compiler_params=_params(("parallel",), VMEM_BIG),
        name="mla_in",
    )(x, w_pad, g_q.reshape(1, -1), g_kv.reshape(1, -1), *tabs)


def _mla_q_kernel(cq_ref, wn_ref, wr_ref, wuk_ref, c_ref, sa_ref, sb_ref, q_ref):
    cq = cq_ref[...]
    qn = _dot(cq, wn_ref[0])
    ql = _dot(qn.astype(bf16), wuk_ref[0])
    qr = _rope_tile(_dot(cq, wr_ref[0]), c_ref[...], sa_ref[...], sb_ref[...], MLA_ROPE // 2)
    q_ref[0, :, :MLA_KV_LORA] = ql.astype(bf16)
    q_ref[0, :, MLA_KV_LORA:] = qr[:, :MLA_ROPE].astype(bf16)


def _mla_q(cq, wn, wr, wuk, tabs, *, tm=512):
    m = cq.shape[0]
    tab_spec = pl.BlockSpec((tm, LANES), lambda i, h: (i, 0))
    return pl.pallas_call(
        _mla_q_kernel,
        grid=(m // tm, MLA_HEADS),
        in_specs=[pl.BlockSpec((tm, MLA_Q_LORA), lambda i, h: (i, 0)),
                  pl.BlockSpec((1, MLA_Q_LORA, MLA_NOPE), lambda i, h: (h, 0, 0)),
                  pl.BlockSpec((1, MLA_Q_LORA, LANES), lambda i, h: (h, 0, 0)),
                  pl.BlockSpec((1, MLA_NOPE, MLA_KV_LORA), lambda i, h: (h, 0, 0)),
                  tab_spec, tab_spec, tab_spec],
        out_specs=pl.BlockSpec((1, tm, MLA_CACHE), lambda i, h: (h, i, 0)),
        out_shape=jax.ShapeDtypeStruct((MLA_HEADS, m, MLA_CACHE), bf16),
        compiler_params=_params(("parallel", "arbitrary")),
        name="mla_q",
    )(cq, wn, wr, wuk, *tabs)


MLA_TQ = 128
MLA_TK = 512


def _mla_flash_kernel(q_ref, k_ref, o_ref, m_ref, l_ref, acc_ref):
    qi = pl.program_id(1)
    ki = pl.program_id(2)
    rows = MLA_HEADS * MLA_TQ
    last_k = (qi * MLA_TQ + MLA_TQ - 1) // MLA_TK

    @pl.when(ki == 0)
    def _():
        m_ref[...] = jnp.full_like(m_ref, NEG_BIG)
        l_ref[...] = jnp.zeros_like(l_ref)
        acc_ref[...] = jnp.zeros_like(acc_ref)

    @pl.when(ki <= last_k)
    def _():
        q = q_ref[...].reshape(rows, MLA_CACHE)
        k = k_ref[...]
        s = _nt(q, k) * MLA_SCALE
        qpos = qi * MLA_TQ + (lax.broadcasted_iota(jnp.int32, (rows, MLA_TK), 0) & (MLA_TQ - 1))
        kpos = ki * MLA_TK + lax.broadcasted_iota(jnp.int32, (rows, MLA_TK), 1)
        s = jnp.where(kpos <= qpos, s, NEG_BIG)
        m_prev = m_ref[...]
        m_new = jnp.maximum(m_prev, jnp.max(s, -1, keepdims=True))
        alpha = jnp.exp(m_prev - m_new)
        p = jnp.exp(s - m_new)
        l_ref[...] = alpha * l_ref[...] + jnp.sum(p, -1, keepdims=True)
        acc_ref[...] = alpha * acc_ref[...] + _dot(p.astype(bf16), k[:, :MLA_KV_LORA])
        m_ref[...] = m_new

    @pl.when(ki == pl.num_programs(2) - 1)
    def _():
        o = acc_ref[...] / l_ref[...]
        o_ref[...] = o.astype(bf16).reshape(MLA_HEADS, MLA_TQ, MLA_KV_LORA)


def _mla_flash(q_cat, rows_bf):
    nq = SEQ // MLA_TQ
    nk = SEQ // MLA_TK

    def k_map(b, qi, ki):
        return (b * nk + jnp.minimum(ki, (qi * MLA_TQ + MLA_TQ - 1) // MLA_TK), 0)

    return pl.pallas_call(
        _mla_flash_kernel,
        grid=(BATCH, nq, nk),
        in_specs=[pl.BlockSpec((MLA_HEADS, MLA_TQ, MLA_CACHE), lambda b, qi, ki: (0, b * nq + qi, 0)),
                  pl.BlockSpec((MLA_TK, MLA_CACHE), k_map)],
        out_specs=pl.BlockSpec((MLA_HEADS, MLA_TQ, MLA_KV_LORA), lambda b, qi, ki: (0, b * nq + qi, 0)),
        out_shape=jax.ShapeDtypeStruct((MLA_HEADS, NP_TOK, MLA_KV_LORA), bf16),
        scratch_shapes=[pltpu.VMEM((MLA_HEADS * MLA_TQ, 1), f32),
                        pltpu.VMEM((MLA_HEADS * MLA_TQ, 1), f32),
                        pltpu.VMEM((MLA_HEADS * MLA_TQ, MLA_KV_LORA), f32)],
        compiler_params=_params(("parallel", "parallel", "arbitrary"), VMEM_BIG),
        name="mla_flash",
    )(q_cat, rows_bf)


def _mla_uv_kernel(o_ref, w_ref, v_ref):
    v_ref[...] = _dot(o_ref[0], w_ref[0]).astype(bf16)


def _mla_uv(o_lat, wuv, *, tm=512):
    m = o_lat.shape[1]
    return pl.pallas_call(
        _mla_uv_kernel,
        grid=(m // tm, MLA_HEADS),
        in_specs=[pl.BlockSpec((1, tm, MLA_KV_LORA), lambda i, h: (h, i, 0)),
                  pl.BlockSpec((1, MLA_KV_LORA, MLA_V), lambda i, h: (h, 0, 0))],
        out_specs=pl.BlockSpec((tm, MLA_V), lambda i, h: (i, h)),
        out_shape=jax.ShapeDtypeStruct((m, MLA_HEADS * MLA_V), bf16),
        compiler_params=_params(("parallel", "arbitrary")),
        name="mla_uv",
    )(o_lat, wuv)


def _paged_fetch(pt_ref, page_src, buf_ref, sem_ref, pages):
    n_chunks = pl.num_programs(1)
    step = pl.program_id(0) * n_chunks + pl.program_id(1)
    total = pl.num_programs(0) * n_chunks
    slot = lax.rem(step, 2)

    def copies(s, sl):
        return [pltpu.make_async_copy(page_src(pt_ref[s * pages + p]), buf_ref.at[sl, p], sem_ref.at[sl])
                for p in range(pages)]

    @pl.when(step == 0)
    def _():
        for cp in copies(0, 0):
            cp.start()

    @pl.when(step + 1 < total)
    def _():
        for cp in copies(step + 1, 1 - slot):
            cp.start()

    for cp in copies(step, slot):
        cp.wait()
    return slot


MLA_PG = 8
MLA_SROWS = MLA_HEADS * DEC_SEQ
NEW_PAD = LANES


def _mla_paged_kernel(pt_ref, q_ref, new_ref, cache_ref, o_ref, buf_ref, sem_ref, m_ref, l_ref, acc_ref,
                      *, layer):
    c = pl.program_id(1)
    slot = _paged_fetch(pt_ref, lambda pg: cache_ref.at[layer, pg], buf_ref, sem_ref, MLA_PG)
    q = q_ref[0]

    @pl.when(c == 0)
    def _():
        kn = new_ref[0].astype(bf16)
        s = _nt(q, kn) * MLA_SCALE
        tq = lax.broadcasted_iota(jnp.int32, s.shape, 0) & (DEC_SEQ - 1)
        tk = lax.broadcasted_iota(jnp.int32, s.shape, 1)
        s = jnp.where(tk <= tq, s, NEG_BIG)
        m0 = jnp.max(s, -1, keepdims=True)
        p = jnp.exp(s - m0)
        m_ref[...] = m0
        l_ref[...] = jnp.sum(p, -1, keepdims=True)
        acc_ref[...] = _dot(p.astype(bf16), kn[:, :MLA_KV_LORA])

    kts = [buf_ref[slot, pg].astype(bf16) for pg in range(MLA_PG)]
    s = jnp.concatenate([_dot(q, kt) for kt in kts], axis=1) * MLA_SCALE
    m_prev = m_ref[...]
    m_new = jnp.maximum(m_prev, jnp.max(s, -1, keepdims=True))
    alpha = jnp.exp(m_prev - m_new)
    p = jnp.exp(s - m_new)
    l_ref[...] = alpha * l_ref[...] + jnp.sum(p, -1, keepdims=True)
    pb = p.astype(bf16)
    pv = _nt(pb[:, :PAGE_SIZE], kts[0][:MLA_KV_LORA])
    for pg in range(1, MLA_PG):
        pv = pv + _nt(pb[:, pg * PAGE_SIZE:(pg + 1) * PAGE_SIZE], kts[pg][:MLA_KV_LORA])
    acc_ref[...] = alpha * acc_ref[...] + pv
    m_ref[...] = m_new

    @pl.when(c == pl.num_programs(1) - 1)
    def _():
        o_ref[0] = (acc_ref[...] / l_ref[...]).astype(bf16)


def _mla_paged(pt_flat, q_s, rows_new, cache, layer):
    nc = N_PAGES // MLA_PG
    kern = functools.partial(_mla_paged_kernel, layer=layer)
    return pl.pallas_call(
        kern,
        grid_spec=pltpu.PrefetchScalarGridSpec(
            num_scalar_prefetch=1,
            grid=(DEC_BATCH, nc),
            in_specs=[pl.BlockSpec((1, MLA_SROWS, MLA_CACHE), lambda b, c, pt: (b, 0, 0)),
                      pl.BlockSpec((1, NEW_PAD, MLA_CACHE), lambda b, c, pt: (b, 0, 0)),
                      pl.BlockSpec(memory_space=pl.ANY)],
            out_specs=pl.BlockSpec((1, MLA_SROWS, MLA_KV_LORA), lambda b, c, pt: (b, 0, 0)),
            scratch_shapes=[pltpu.VMEM((2, MLA_PG, MLA_CACHE, PAGE_SIZE), f32),
                            pltpu.SemaphoreType.DMA((2,)),
                            pltpu.VMEM((MLA_SROWS, 1), f32),
                            pltpu.VMEM((MLA_SROWS, 1), f32),
                            pltpu.VMEM((MLA_SROWS, MLA_KV_LORA), f32)]),
        out_shape=jax.ShapeDtypeStruct((DEC_BATCH, MLA_SROWS, MLA_KV_LORA), bf16),
        compiler_params=_params(("arbitrary", "arbitrary"), VMEM_BIG),
        name="mla_paged",
    )(pt_flat, q_s, rows_new, cache)


def _sortable(x):
    b = pltpu.bitcast(x, jnp.int32)
    return b ^ ((b >> 31) & 0x7FFFFFFF)


def _count(mask):
    return jnp.sum(jnp.where(mask, 1.0, 0.0), -1, keepdims=True)


def _topk_mask(si, col, kk, idx_bits):
    kf = float(kk)
    tau0 = jnp.where(_count(si >= 0) >= kf, 0, INT_MIN).astype(jnp.int32)

    def tau_step(i, tau):
        cand = tau | jnp.left_shift(jnp.int32(1), 30 - i)
        return jnp.where(_count(si >= cand) >= kf, cand, tau)

    tau = lax.fori_loop(0, 31, tau_step, tau0)
    gt = si > tau
    eq = si == tau
    need = kf - _count(gt)
    surplus = jnp.max(_count(eq) - need)

    def cut_search(_):
        def cut_step(i, cut):
            cand = cut | jnp.left_shift(jnp.int32(1), idx_bits - 1 - i)
            return jnp.where(_count(eq & (col < cand)) < need, cand, cut)
        return lax.fori_loop(0, idx_bits, cut_step, jnp.zeros_like(tau))

    def no_cut(_):
        return jnp.full_like(tau, (1 << idx_bits) - 1)

    cut = lax.cond(surplus > 0.0, cut_search, no_cut, 0)
    return gt | (eq & (col <= cut))


DSA_W = 5376
DSA_COL_IQ = 0
DSA_COL_Q = 2048
DSA_COL_KV = 4096
DSA_COL_IK = 5120
DSA_COL_IW = 5248
DSA_TN = 768
DSA_TQ = 128


def _dsa_in_kernel(x_ref, w_ref, c_ref, sa_ref, sb_ref, o_ref, xb_ref):
    j = pl.program_id(1)

    @pl.when(j == 0)
    def _():
        xb_ref[...] = x_ref[...].astype(bf16)

    h = _dot(xb_ref[...], w_ref[...])
    c, sa, sb = c_ref[...], sa_ref[...], sb_ref[...]
    for s in range(DSA_TN // LANES):
        tile = jnp.full((h.shape[0], LANES), j * (DSA_TN // LANES) + s, jnp.int32)
        x = h[:, s * LANES:(s + 1) * LANES]
        roped = _rope_tile(x, c, sa, sb, DSA_ROT // 2)
        is_v = (tile >= (DSA_COL_KV + 512) // LANES) & (tile < DSA_COL_IK // LANES)
        is_w = tile == DSA_COL_IW // LANES
        y = jnp.where(is_v, x, roped)
        y = jnp.where(is_w, x * DSA_IW_SCALE, y)
        o_ref[:, s * LANES:(s + 1) * LANES] = y


def _dsa_in(x, w, tabs, *, tm=512):
    m, d = x.shape
    tab_spec = pl.BlockSpec((tm, LANES), lambda i, j: (i, 0))
    return pl.pallas_call(
        _dsa_in_kernel,
        grid=(m // tm, DSA_W // DSA_TN),
        in_specs=[pl.BlockSpec((tm, d), lambda i, j: (i, 0)),
                  pl.BlockSpec((d, DSA_TN), lambda i, j: (0, j)),
                  tab_spec, tab_spec, tab_spec],
        out_specs=pl.BlockSpec((tm, DSA_TN), lambda i, j: (i, j)),
        out_shape=jax.ShapeDtypeStruct((m, DSA_W), f32),
        scratch_shapes=[pltpu.VMEM((tm, d), bf16)],
        compiler_params=_params(("parallel", "arbitrary"), VMEM_BIG),
        name="dsa_in",
    )(x, w, *tabs)


def _dsa_prompt_kernel(iq_ref, q_ref, iw_ref, ik_ref, kv_ref, o_ref, ikb_ref, kvb_ref):
    qi = pl.program_id(1)

    @pl.when(qi == 0)
    def _():
        ikb_ref[...] = ik_ref[...].astype(bf16)
        kvb_ref[...] = kv_ref[...].astype(bf16)

    ikb = ikb_ref[...]
    iw = iw_ref[...]
    sc = jnp.zeros((DSA_TQ, SEQ), f32)
    for h in range(DSA_IDX_HEADS):
        s = _nt(iq_ref[:, h * LANES:(h + 1) * LANES].astype(bf16), ikb)
        sc = sc + iw[:, h:h + 1] * jnp.maximum(s, 0.0)

    row = lax.broadcasted_iota(jnp.int32, (DSA_TQ, SEQ), 0)
    col = lax.broadcasted_iota(jnp.int32, (DSA_TQ, SEQ), 1)
    causal = col <= qi * DSA_TQ + row
    si = jnp.where(causal, _sortable(sc), INT_MIN)
    sel = _topk_mask(si, col, DSA_TOPK_MAX, 11) & causal

    group = DSA_HEADS // DSA_KV_HEADS
    for h in range(DSA_HEADS):
        j = h // group
        kj = kvb_ref[:, j * DSA_HD:(j + 1) * DSA_HD]
        vj = kvb_ref[:, (DSA_KV_HEADS + j) * DSA_HD:(DSA_KV_HEADS + j + 1) * DSA_HD]
        s = _nt(q_ref[:, h * DSA_HD:(h + 1) * DSA_HD].astype(bf16), kj) * DSA_SCALE
        s = jnp.where(sel, s, NEG_BIG)
        p = jnp.exp(s - jnp.max(s, -1, keepdims=True))
        o = _dot(p.astype(bf16), vj) / jnp.sum(p, -1, keepdims=True)
        o_ref[:, h * DSA_HD:(h + 1) * DSA_HD] = o.astype(bf16)


def _dsa_prompt(hd):
    nq = SEQ // DSA_TQ
    return pl.pallas_call(
        _dsa_prompt_kernel,
        grid=(BATCH, nq),
        in_specs=[pl.BlockSpec((DSA_TQ, 2048), lambda b, qi: (b * nq + qi, DSA_COL_IQ // 2048)),
                  pl.BlockSpec((DSA_TQ, 2048), lambda b, qi: (b * nq + qi, DSA_COL_Q // 2048)),
                  pl.BlockSpec((DSA_TQ, LANES), lambda b, qi: (b * nq + qi, DSA_COL_IW // LANES)),
                  pl.BlockSpec((SEQ, LANES), lambda b, qi: (b, DSA_COL_IK // LANES)),
                  pl.BlockSpec((SEQ, 1024), lambda b, qi: (b, DSA_COL_KV // 1024))],
        out_specs=pl.BlockSpec((DSA_TQ, DSA_HEADS * DSA_HD), lambda b, qi: (b * nq + qi, 0)),
        out_shape=jax.ShapeDtypeStruct((NP_TOK, DSA_HEADS * DSA_HD), bf16),
        scratch_shapes=[pltpu.VMEM((SEQ, LANES), bf16), pltpu.VMEM((SEQ, 1024), bf16)],
        compiler_params=_params(("parallel", "arbitrary"), VMEM_BIG),
        name="dsa_prompt",
    )(hd, hd, hd, hd, hd)


DSI_PG = 16
DSI_W = PAST_LEN + LANES
DSI_ROWS = 2 * DEC_SEQ


def _dsa_idx_kernel(pt_ref, iq_ref, iw_ref, ikn_ref, cache_ref, mask_ref, buf_ref, sem_ref, sc_ref, *, layer):
    c = pl.program_id(1)
    slot = _paged_fetch(pt_ref, lambda pg: cache_ref.at[layer, pg], buf_ref, sem_ref, DSI_PG)
    iq = iq_ref[0]
    iw = iw_ref[0]
    n = DSI_PG * PAGE_SIZE
    k = buf_ref[slot].reshape(n, DSA_IDX_DIM).astype(bf16)
    s = jnp.maximum(_nt(iq, k), 0.0) * iw
    sc_ref[:, pl.ds(pl.multiple_of(c * n, LANES), n)] = jnp.sum(s.reshape(DSI_ROWS, DSA_IDX_HEADS, n), axis=1)

    @pl.when(c == pl.num_programs(1) - 1)
    def _():
        sn = jnp.maximum(_nt(iq, ikn_ref[0].astype(bf16)), 0.0) * iw
        sc_ref[:, PAST_LEN:] = jnp.sum(sn.reshape(DSI_ROWS, DSA_IDX_HEADS, LANES), axis=1)
        sc = sc_ref[...]
        t = lax.broadcasted_iota(jnp.int32, sc.shape, 0) & (DEC_SEQ - 1)
        col = lax.broadcasted_iota(jnp.int32, sc.shape, 1)
        valid = col <= PAST_LEN + t
        si = jnp.where(valid, _sortable(sc), INT_MIN)
        sel = _topk_mask(si, col, DSA_TOPK_MAX, 14) & valid
        mask_ref[0] = jnp.where(sel, 1.0, 0.0)


def _dsa_idx(pt_flat, iq_s, iw_s, ik_new, cache, layer):
    nc = N_PAGES // DSI_PG
    kern = functools.partial(_dsa_idx_kernel, layer=layer)
    rows = DSI_ROWS * DSA_IDX_HEADS
    return pl.pallas_call(
        kern,
        grid_spec=pltpu.PrefetchScalarGridSpec(
            num_scalar_prefetch=1,
            grid=(DEC_BATCH, nc),
            in_specs=[pl.BlockSpec((1, rows, DSA_IDX_DIM), lambda b, c, pt: (b, 0, 0)),
                      pl.BlockSpec((1, rows, 1), lambda b, c, pt: (b, 0, 0)),
                      pl.BlockSpec((1, LANES, DSA_IDX_DIM), lambda b, c, pt: (b, 0, 0)),
                      pl.BlockSpec(memory_space=pl.ANY)],
            out_specs=pl.BlockSpec((1, DSI_ROWS, DSI_W), lambda b, c, pt: (b, 0, 0)),
            scratch_shapes=[pltpu.VMEM((2, DSI_PG, PAGE_SIZE, DSA_IDX_DIM), f32),
                            pltpu.SemaphoreType.DMA((2,)),
                            pltpu.VMEM((DSI_ROWS, DSI_W), f32)]),
        out_shape=jax.ShapeDtypeStruct((DEC_BATCH, DSI_ROWS, DSI_W), f32),
        compiler_params=_params(("arbitrary", "arbitrary"), VMEM_BIG),
        name="dsa_idx",
    )(pt_flat, iq_s, iw_s, ik_new, cache)


DSP_PG = 8
DSP_ROWS = DSA_HEADS * DEC_SEQ
KV_SLICES = 2 * DSA_KV_HEADS
PAGE_ROWS = PAGE_SIZE * KV_SLICES


def _dsa_paged_kernel(pt_ref, q_ref, mask_ref, kvn_ref, cache_ref, o_ref, buf_ref, sem_ref,
                      m_ref, l_ref, acc_ref, *, layer):
    c = pl.program_id(1)
    slot = _paged_fetch(pt_ref, lambda pg: cache_ref.at[layer, pg], buf_ref, sem_ref, DSP_PG)
    q = q_ref[0]
    shape = (DSP_ROWS, PAGE_ROWS)
    row = lax.broadcasted_iota(jnp.int32, shape, 0)
    lane = lax.broadcasted_iota(jnp.int32, shape, 1)
    head_ok = (lane & (KV_SLICES - 1)) == (row >> 4)
    er = lax.broadcasted_iota(jnp.int32, (PAGE_SIZE, PAGE_ROWS), 0)
    el = lax.broadcasted_iota(jnp.int32, (PAGE_SIZE, PAGE_ROWS), 1)
    expand = jnp.where((el >> 3) == er, 1.0, 0.0).astype(bf16)

    def attend(x, selp, m_prev, l_prev, acc_prev):
        s = _nt(q, x) * DSA_SCALE
        e = _dot(selp.astype(bf16), expand)
        ok = head_ok & (jnp.concatenate([e] * (DSP_ROWS // DSI_ROWS), axis=0) > 0.5)
        s = jnp.where(ok, s, NEG_BIG)
        m_new = jnp.maximum(m_prev, jnp.max(s, -1, keepdims=True))
        alpha = jnp.exp(m_prev - m_new)
        p = jnp.where(ok, jnp.exp(s - m_new), 0.0)
        l_new = alpha * l_prev + jnp.sum(p, -1, keepdims=True)
        pv = _dot(pltpu.roll(p, DSA_KV_HEADS, 1).astype(bf16), x)
        return m_new, l_new, alpha * acc_prev + pv

    @pl.when(c == 0)
    def _():
        m0 = jnp.full((DSP_ROWS, 1), NEG_BIG, f32)
        l0 = jnp.zeros((DSP_ROWS, 1), f32)
        a0 = jnp.zeros((DSP_ROWS, DSA_HD), f32)
        m1, l1, a1 = attend(kvn_ref[0].astype(bf16), mask_ref[0, :, PAST_LEN:], m0, l0, a0)
        m_ref[...] = m1
        l_ref[...] = l1
        acc_ref[...] = a1

    def page_step(p, carry):
        x = buf_ref[slot, p].astype(bf16)
        off = pl.multiple_of((c * DSP_PG + p) * PAGE_SIZE, LANES)
        return attend(x, mask_ref[0, :, pl.ds(off, PAGE_SIZE)], *carry)

    m1, l1, a1 = lax.fori_loop(0, DSP_PG, page_step, (m_ref[...], l_ref[...], acc_ref[...]))
    m_ref[...] = m1
    l_ref[...] = l1
    acc_ref[...] = a1

    @pl.when(c == pl.num_programs(1) - 1)
    def _():
        o_ref[0] = (a1 / l1).astype(bf16)


def _dsa_paged(pt_flat, q_s, mask, kv_new, cache, layer):
    nc = N_PAGES // DSP_PG
    kern = functools.partial(_dsa_paged_kernel, layer=layer)
    return pl.pallas_call(
        kern,
        grid_spec=pltpu.PrefetchScalarGridSpec(
            num_scalar_prefetch=1,
            grid=(DEC_BATCH, nc),
            in_specs=[pl.BlockSpec((1, DSP_ROWS, DSA_HD), lambda b, c, pt: (b, 0, 0)),
                      pl.BlockSpec((1, DSI_ROWS, DSI_W), lambda b, c, pt: (b, 0, 0)),
                      pl.BlockSpec((1, PAGE_ROWS, DSA_HD), lambda b, c, pt: (b, 0, 0)),
                      pl.BlockSpec(memory_space=pl.ANY)],
            out_specs=pl.BlockSpec((1, DSP_ROWS, DSA_HD), lambda b, c, pt: (b, 0, 0)),
            scratch_shapes=[pltpu.VMEM((2, DSP_PG, PAGE_ROWS, DSA_HD), f32),
                            pltpu.SemaphoreType.DMA((2,)),
                            pltpu.VMEM((DSP_ROWS, 1), f32),
                            pltpu.VMEM((DSP_ROWS, 1), f32),
                            pltpu.VMEM((DSP_ROWS, DSA_HD), f32)]),
        out_shape=jax.ShapeDtypeStruct((DEC_BATCH, DSP_ROWS, DSA_HD), bf16),
        compiler_params=_params(("arbitrary", "arbitrary"), VMEM_BIG),
        name="dsa_paged",
    )(pt_flat, q_s, mask, kv_new, cache)


GDN_MAIN = GDN_CONV_DIM + GDN_VD
GDN_HG = 8
GDN_NG = GDN_HV // GDN_HG
CONV_TC = 512


def _sigmoid(x):
    return 1.0 / (1.0 + jnp.exp(-x))


def _gdn_ba_kernel(x_ref, w_ref, alog_ref, dtb_ref, o_ref):
    h = _dot(x_ref[...].astype(bf16), w_ref[...])
    a = h + dtb_ref[...]
    softplus = jnp.maximum(a, 0.0) + jnp.log1p(jnp.exp(-jnp.abs(a)))
    g = -jnp.exp(alog_ref[...]) * softplus
    lane = lax.broadcasted_iota(jnp.int32, h.shape, 1)
    o_ref[...] = jnp.where(lane < GDN_HV, _sigmoid(h), g)


def _gdn_ba(x, w_ba, a_log, dt_bias, *, tm=512):
    m, d = x.shape
    pad = jnp.zeros((GDN_HV,), f32)
    zeros = jnp.zeros((LANES - 2 * GDN_HV,), f32)
    alog = jnp.concatenate([pad, a_log.astype(f32), zeros]).reshape(1, LANES)
    dtb = jnp.concatenate([pad, dt_bias.astype(f32), zeros]).reshape(1, LANES)
    return pl.pallas_call(
        _gdn_ba_kernel,
        grid=(m // tm,),
        in_specs=[pl.BlockSpec((tm, d), lambda i: (i, 0)),
                  pl.BlockSpec((d, LANES), lambda i: (0, 0)),
                  pl.BlockSpec((1, LANES), lambda i: (0, 0)),
                  pl.BlockSpec((1, LANES), lambda i: (0, 0))],
        out_specs=pl.BlockSpec((tm, LANES), lambda i: (i, 0)),
        out_shape=jax.ShapeDtypeStruct((m, LANES), f32),
        compiler_params=_params(("parallel",)),
        name="gdn_ba",
    )(x, w_ba, alog, dtb)


def _conv_finish(acc, ci, o_ref):
    y = acc * _sigmoid(acc)
    heads_per_tile = CONV_TC // GDN_DK
    q_tiles = GDN_QK // CONV_TC
    for s in range(heads_per_tile):
        seg = y[:, s * GDN_DK:(s + 1) * GDN_DK]
        n = seg * lax.rsqrt(jnp.sum(seg * seg, -1, keepdims=True) + RMS_EPS)
        tile = jnp.full(seg.shape, ci, jnp.int32)
        n = jnp.where(tile < q_tiles, n * (GDN_DK ** -0.5), n)
        o_ref[:, s * GDN_DK:(s + 1) * GDN_DK] = jnp.where(tile < 2 * q_tiles, n, seg)


def _conv_prompt_kernel(x_ref, halo_ref, w_ref, o_ref, ext_ref, *, tt):
    ti = pl.program_id(1)
    ci = pl.program_id(2)
    first = jnp.full(halo_ref.shape, ti, jnp.int32) == 0
    ext_ref[0:8, :] = jnp.where(first, 0.0, halo_ref[...])
    ext_ref[8:, :] = x_ref[...]
    w = w_ref[...]
    acc = ext_ref[pl.ds(8 - (GDN_CONV - 1), tt), :] * w[0:1, :]
    for j in range(1, GDN_CONV):
        acc = acc + ext_ref[pl.ds(8 - (GDN_CONV - 1) + j, tt), :] * w[j:j + 1, :]
    _conv_finish(acc, ci, o_ref)


def _conv_prompt(hg, w_conv_t, *, tt=512):
    nt = SEQ // tt
    kern = functools.partial(_conv_prompt_kernel, tt=tt)

    def halo_map(b, ti, ci):
        return (jnp.maximum(b * (SEQ // 8) + ti * (tt // 8) - 1, 0), ci)

    return pl.pallas_call(
        kern,
        grid=(BATCH, nt, GDN_CONV_DIM // CONV_TC),
        in_specs=[pl.BlockSpec((tt, CONV_TC), lambda b, ti, ci: (b * nt + ti, ci)),
                  pl.BlockSpec((8, CONV_TC), halo_map),
                  pl.BlockSpec((GDN_CONV, CONV_TC), lambda b, ti, ci: (0, ci))],
        out_specs=pl.BlockSpec((tt, CONV_TC), lambda b, ti, ci: (b * nt + ti, ci)),
        out_shape=jax.ShapeDtypeStruct((NP_TOK, GDN_CONV_DIM), f32),
        scratch_shapes=[pltpu.VMEM((tt + 8, CONV_TC), f32)],
        compiler_params=_params(("parallel", "parallel", "parallel")),
        name="conv_prompt",
    )(hg, hg, w_conv_t)


def _conv_sample_kernel(x_ref, st_ref, w_ref, o_ref):
    ci = pl.program_id(0)
    w = w_ref[...]
    xp = [st_ref[i] for i in range(GDN_CONV - 1)] + [x_ref[i] for i in range(DEC_SEQ)]
    for t in range(DEC_SEQ):
        acc = xp[t] * w[0:1, :]
        for j in range(1, GDN_CONV):
            acc = acc + xp[t + j] * w[j:j + 1, :]
        _conv_finish(acc, ci, o_ref.at[t])


def _conv_sample(x_t, st_t, w_conv_t):
    return pl.pallas_call(
        _conv_sample_kernel,
        grid=(GDN_CONV_DIM // CONV_TC,),
        in_specs=[pl.BlockSpec((DEC_SEQ, DEC_BATCH, CONV_TC), lambda ci: (0, 0, ci)),
                  pl.BlockSpec((GDN_CONV - 1, DEC_BATCH, CONV_TC), lambda ci: (0, 0, ci)),
                  pl.BlockSpec((GDN_CONV, CONV_TC), lambda ci: (0, ci))],
        out_specs=pl.BlockSpec((DEC_SEQ, DEC_BATCH, CONV_TC), lambda ci: (0, 0, ci)),
        out_shape=jax.ShapeDtypeStruct((DEC_SEQ, DEC_BATCH, GDN_CONV_DIM), f32),
        compiler_params=_params(("parallel",)),
        name="conv_sample",
    )(x_t, st_t, w_conv_t)


def _gdn_chunk_kernel(*refs, chunk, n_chunks, has_s0):
    if has_s0:
        q_ref, k_ref, v_ref, z_ref, gb_ref, gt_ref, gout_ref, s0_ref, o_ref, sfin_ref, s_ref = refs
    else:
        q_ref, k_ref, v_ref, z_ref, gb_ref, gt_ref, gout_ref, o_ref, sfin_ref, s_ref = refs
    tb = pl.program_id(2)

    @pl.when(tb == 0)
    def _():
        if has_s0:
            s_ref[...] = s0_ref[0]
        else:
            s_ref[...] = jnp.zeros_like(s_ref)

    ii = lax.broadcasted_iota(jnp.int32, (chunk, chunk), 0)
    jj = lax.broadcasted_iota(jnp.int32, (chunk, chunk), 1)
    lower = ii >= jj
    strict = ii > jj
    tri = jnp.where(lower, 1.0, 0.0)
    tri_t = jnp.where(ii <= jj, 1.0, 0.0)
    eye = jnp.where(ii == jj, 1.0, 0.0)
    gout = gout_ref[...]
    levels = chunk.bit_length() - 1

    def chunk_step(ci, carry):
        r0 = pl.multiple_of(ci * chunk, chunk)
        gb = gb_ref[0, ci]
        gcum_all = _dot_hi(tri, gb)
        gcum_t = _dot_hi(gt_ref[0, ci], tri_t)
        for i in range(GDN_HG):
            hk = i // (GDN_HV // GDN_HK)
            qh = q_ref[pl.ds(r0, chunk), hk * GDN_DK:(hk + 1) * GDN_DK]
            kh = k_ref[pl.ds(r0, chunk), hk * GDN_DK:(hk + 1) * GDN_DK]
            vh = v_ref[pl.ds(r0, chunk), i * GDN_DV:(i + 1) * GDN_DV]
            gc = gcum_all[:, i:i + 1]
            gr = gcum_t[i:i + 1, :]
            bc = gb[:, GDN_HG + i:GDN_HG + i + 1]
            decay = jnp.where(lower, jnp.exp(jnp.where(lower, gc - gr, 0.0)), 0.0)
            kb = kh * bc
            vb = vh * bc
            khb = kh.astype(bf16)
            lmat = jnp.where(strict, _nt(kb.astype(bf16), khb) * decay, 0.0)
            tinv = eye - lmat
            lp = lmat
            for _ in range(levels - 1):
                lp = _dot_hi(lp, lp)
                tinv = tinv + _dot_hi(tinv, lp)
            egc = jnp.exp(gc)
            sol = _dot_hi(tinv, jnp.concatenate([vb, kb * egc], axis=1))
            u = sol[:, :GDN_DV]
            w = sol[:, GDN_DV:]
            a_intra = jnp.where(lower, _nt(qh.astype(bf16), khb) * decay, 0.0)
            s = s_ref[i]
            sb = s.astype(bf16)
            v_new = u - _dot(w.astype(bf16), sb)
            vnb = v_new.astype(bf16)
            o = _dot((qh * egc).astype(bf16), sb) + _dot(a_intra.astype(bf16), vnb)
            g_last = gc[chunk - 1:chunk, :]
            s_ref[i] = s * jnp.exp(g_last) + _tn((kh * jnp.exp(g_last - gc)).astype(bf16), vnb)
            on = o * lax.rsqrt(jnp.mean(o * o, -1, keepdims=True) + RMS_EPS) * gout
            zh = z_ref[pl.ds(r0, chunk), i * GDN_DV:(i + 1) * GDN_DV]
            o_ref[pl.ds(r0, chunk), i * GDN_DV:(i + 1) * GDN_DV] = (on * (zh * _sigmoid(zh))).astype(bf16)
        return carry

    lax.fori_loop(0, n_chunks, chunk_step, 0)

    @pl.when(tb == pl.num_programs(2) - 1)
    def _():
        sfin_ref[0] = s_ref[...]


def _gdn_chunks(qkv, z, z_col0, gb, gt, g_out, s0, *, n_seq, seq_len, chunk, block):
    rows = n_seq * seq_len
    nb = seq_len // block
    n_chunks = block // chunk
    has_s0 = s0 is not None
    qk_w = GDN_HG // (GDN_HV // GDN_HK) * GDN_DK
    v_w = GDN_HG * GDN_DV
    kern = functools.partial(_gdn_chunk_kernel, chunk=chunk, n_chunks=n_chunks, has_s0=has_s0)
    in_specs = [pl.BlockSpec((block, qk_w), lambda s, g, t: (s * nb + t, g)),
                pl.BlockSpec((block, qk_w), lambda s, g, t: (s * nb + t, GDN_QK // qk_w + g)),
                pl.BlockSpec((block, v_w), lambda s, g, t: (s * nb + t, 2 * GDN_QK // v_w + g)),
                pl.BlockSpec((block, v_w), lambda s, g, t: (s * nb + t, z_col0 // v_w + g)),
                pl.BlockSpec((1, n_chunks, chunk, LANES), lambda s, g, t: (g, s * nb + t, 0, 0)),
                pl.BlockSpec((1, n_chunks, GDN_HG, chunk), lambda s, g, t: (g, s * nb + t, 0, 0)),
                pl.BlockSpec((1, GDN_DV), lambda s, g, t: (0, 0))]
    args = [qkv, qkv, qkv, z, gb, gt, g_out.reshape(1, GDN_DV)]
    if has_s0:
        in_specs.append(pl.BlockSpec((1, GDN_HG, GDN_DK, GDN_DV), lambda s, g, t: (s, g, 0, 0)))
        args.append(s0)
    return pl.pallas_call(
        kern,
        grid=(n_seq, GDN_NG, nb),
        in_specs=in_specs,
        out_specs=[pl.BlockSpec((block, v_w), lambda s, g, t: (s * nb + t, g)),
                   pl.BlockSpec((1, GDN_HG, GDN_DK, GDN_DV), lambda s, g, t: (s, g, 0, 0))],
        out_shape=[jax.ShapeDtypeStruct((rows, GDN_VD), bf16),
                   jax.ShapeDtypeStruct((n_seq, GDN_HV, GDN_DK, GDN_DV), f32)],
        scratch_shapes=[pltpu.VMEM((GDN_HG, GDN_DK, GDN_DV), f32)],
        compiler_params=_params(("parallel", "parallel", "arbitrary"), VMEM_BIG),
        name="gdn_chunks",
    )(*args)


def _gate_tables(bg, n_seq, seq_len, chunk):
    n_ch = n_seq * seq_len // chunk
    beta = bg[:, :GDN_HV].reshape(n_ch, chunk, GDN_NG, GDN_HG)
    g = bg[:, GDN_HV:2 * GDN_HV].reshape(n_ch, chunk, GDN_NG, GDN_HG)
    gb = jnp.concatenate([g, beta, jnp.zeros((n_ch, chunk, GDN_NG, LANES - 2 * GDN_HG), f32)], -1)
    return gb.transpose(2, 0, 1, 3), g.transpose(2, 0, 3, 1)


def _pad_axis(a, axis, size):
    pad = [(0, 0)] * a.ndim
    pad[axis] = (0, size - a.shape[axis])
    return jnp.pad(a, pad)


def _mla_layer(x, j, cache_mla, pt_flat, w_in, g_q, g_kv, w_uq, w_uk, w_uv, w_out, ln_g, ln_b, tabs):
    w_pad = _pad_axis(w_in[j], 1, MLA_IN_PAD).astype(bf16)
    cq, rows = _mla_in(x, w_pad, g_q[j], g_kv[j], tabs)
    wq = w_uq[j].reshape(MLA_Q_LORA, MLA_HEADS, MLA_NOPE + MLA_ROPE)
    wn = wq[:, :, :MLA_NOPE].transpose(1, 0, 2).astype(bf16)
    wr = _pad_axis(wq[:, :, MLA_NOPE:], 2, LANES).transpose(1, 0, 2).astype(bf16)
    wuk = w_uk[j].transpose(1, 2, 0).astype(bf16)
    q_cat = _mla_q(cq, wn, wr, wuk, tabs)
    o_p = _mla_flash(q_cat, rows.astype(bf16))
    q_s = (q_cat[:, NP_TOK:].reshape(MLA_HEADS, DEC_BATCH, DEC_SEQ, MLA_CACHE)
           .transpose(1, 0, 2, 3).reshape(DEC_BATCH, MLA_SROWS, MLA_CACHE))
    rows_new = _pad_axis(rows[NP_TOK:].reshape(DEC_BATCH, DEC_SEQ, MLA_CACHE), 1, NEW_PAD)
    o_s = _mla_paged(pt_flat, q_s, rows_new, cache_mla.transpose(0, 1, 3, 2), j)
    o_s = (o_s.reshape(DEC_BATCH, MLA_HEADS, DEC_SEQ, MLA_KV_LORA)
           .transpose(1, 0, 2, 3).reshape(MLA_HEADS, NS_TOK, MLA_KV_LORA))
    v = _mla_uv(jnp.concatenate([o_p, o_s], 1), w_uv[j].transpose(1, 0, 2).astype(bf16))
    return _proj_ln(v, w_out[j].astype(bf16), x, ln_g, ln_b), rows


GDN_SPAD = 8


def _gdn_layer(x, j, state_s, state_conv, w_in, w_conv, a_log, dt_bias, g_out, w_out, ln_g, ln_b):
    w = w_in[j]
    hg = _matmul(x, w[:, :GDN_MAIN].astype(bf16), tm=512, tn=1024)
    bg = _gdn_ba(x, _pad_axis(w[:, GDN_MAIN:], 1, LANES).astype(bf16), a_log[j], dt_bias[j])
    wct = w_conv[j].T
    conv_p = _conv_prompt(hg, wct)
    mixed_s = hg[NP_TOK:, :GDN_CONV_DIM].reshape(DEC_BATCH, DEC_SEQ, GDN_CONV_DIM)
    conv_s = _conv_sample(mixed_s.transpose(1, 0, 2), state_conv[j].transpose(1, 0, 2), wct)
    conv_s = _pad_axis(conv_s.transpose(1, 0, 2), 1, GDN_SPAD).reshape(DEC_BATCH * GDN_SPAD, GDN_CONV_DIM)
    z_s = _pad_axis(hg[NP_TOK:, GDN_CONV_DIM:].reshape(DEC_BATCH, DEC_SEQ, GDN_VD), 1, GDN_SPAD)
    z_s = z_s.reshape(DEC_BATCH * GDN_SPAD, GDN_VD)
    bg_s = _pad_axis(bg[NP_TOK:].reshape(DEC_BATCH, DEC_SEQ, LANES), 1, GDN_SPAD).reshape(-1, LANES)
    gb_p, gt_p = _gate_tables(bg[:NP_TOK], BATCH, SEQ, GDN_CHUNK)
    gb_s, gt_s = _gate_tables(bg_s, DEC_BATCH, GDN_SPAD, GDN_SPAD)
    o_p, s_p = _gdn_chunks(conv_p, hg, GDN_CONV_DIM, gb_p, gt_p, g_out[j], None,
                           n_seq=BATCH, seq_len=SEQ, chunk=GDN_CHUNK, block=4 * GDN_CHUNK)
    o_s, s_s = _gdn_chunks(conv_s, z_s, 0, gb_s, gt_s, g_out[j], state_s[j],
                           n_seq=DEC_BATCH, seq_len=GDN_SPAD, chunk=GDN_SPAD, block=GDN_SPAD)
    o_s = o_s.reshape(DEC_BATCH, GDN_SPAD, GDN_VD)[:, :DEC_SEQ].reshape(NS_TOK, GDN_VD)
    x1 = _proj_ln(jnp.concatenate([o_p, o_s], 0), w_out[j].astype(bf16), x, ln_g, ln_b)
    buf_p = hg[:NP_TOK, :GDN_CONV_DIM].reshape(BATCH, SEQ, GDN_CONV_DIM)[:, SEQ - (GDN_CONV - 1):]
    buf_s = mixed_s[:, DEC_SEQ - (GDN_CONV - 1):]
    return x1, s_p, buf_p, s_s, buf_s


def _dsa_layer(x, j, cache_kv, cache_idx, pt_flat, w_in, w_out, ln_g, ln_b, tabs):
    w = w_in[j]
    off_k = DSA_HEADS * DSA_HD
    off_iq = off_k + 2 * DSA_KV_HEADS * DSA_HD
    off_ik = off_iq + DSA_IDX_HEADS * DSA_IDX_DIM
    off_iw = off_ik + DSA_IDX_DIM
    w_re = jnp.concatenate([w[:, off_iq:off_ik], w[:, :off_k], w[:, off_k:off_iq], w[:, off_ik:off_iw],
                            _pad_axis(w[:, off_iw:], 1, LANES)], 1).astype(bf16)
    hd = _dsa_in(x, w_re, tabs)
    attn_p = _dsa_prompt(hd)
    hs = hd[NP_TOK:]
    iq_s = hs[:, DSA_COL_IQ:DSA_COL_Q].reshape(DEC_BATCH, DEC_SEQ, DSA_IDX_HEADS, DSA_IDX_DIM)
    iq_s = jnp.concatenate([iq_s, iq_s], 1).reshape(DEC_BATCH, DSI_ROWS * DSA_IDX_HEADS, DSA_IDX_DIM).astype(bf16)
    iw_s = hs[:, DSA_COL_IW:DSA_COL_IW + DSA_IDX_HEADS].reshape(DEC_BATCH, DEC_SEQ, DSA_IDX_HEADS)
    iw_s = jnp.concatenate([iw_s, iw_s], 1).reshape(DEC_BATCH, DSI_ROWS * DSA_IDX_HEADS, 1)
    ik_new = _pad_axis(hs[:, DSA_COL_IK:DSA_COL_IW].reshape(DEC_BATCH, DEC_SEQ, DSA_IDX_DIM), 1, LANES)
    mask = _dsa_idx(pt_flat, iq_s, iw_s, ik_new, cache_idx, j)
    q_s = (hs[:, DSA_COL_Q:DSA_COL_KV].reshape(DEC_BATCH, DEC_SEQ, DSA_HEADS, DSA_HD)
           .transpose(0, 2, 1, 3).reshape(DEC_BATCH, DSP_ROWS, DSA_HD).astype(bf16))
    kv_new = _pad_axis(hs[:, DSA_COL_KV:DSA_COL_IK].reshape(DEC_BATCH, DEC_SEQ * KV_SLICES, DSA_HD), 1, PAGE_ROWS)
    cache_rows = cache_kv.reshape(cache_kv.shape[0], cache_kv.shape[1], PAGE_ROWS, DSA_HD)
    o_s = _dsa_paged(pt_flat, q_s, mask, kv_new, cache_rows, j)
    attn_s = (o_s.reshape(DEC_BATCH, DSA_HEADS, DEC_SEQ, DSA_HD)
              .transpose(0, 2, 1, 3).reshape(NS_TOK, DSA_HEADS * DSA_HD))
    x1 = _proj_ln(jnp.concatenate([attn_p, attn_s], 0), w_out[j].astype(bf16), x, ln_g, ln_b)
    kv = hd[:, DSA_COL_KV:DSA_COL_IK]
    ik = hd[:, DSA_COL_IK:DSA_COL_IW]
    return x1, kv, ik


def kernel(x_prompt, x_sample, cache_mla, state_gdn_S, state_gdn_conv, cache_dsa_kv, cache_dsa_idx, page_table,
           w_mla_in, g_mla_q, g_mla_kv, w_mla_uq, w_mla_uk, w_mla_uv, w_mla_out,
           w_gdn_in, w_gdn_conv, gdn_a_log, gdn_dt_bias, g_gdn_out, w_gdn_out, w_dsa_in, w_dsa_out,
           ln1_g, ln1_b, ln2_g, ln2_b, w_up, w_down):
    x = jnp.concatenate([x_prompt.reshape(NP_TOK, D_MODEL), x_sample.reshape(NS_TOK, D_MODEL)], 0)
    pos = jnp.concatenate([jnp.tile(jnp.arange(SEQ), BATCH), jnp.tile(PAST_LEN + jnp.arange(DEC_SEQ), DEC_BATCH)])
    tabs_mla = _rope_tables(pos, MLA_ROPE)
    tabs_dsa = _rope_tables(pos, DSA_ROT)
    pt_flat = page_table.reshape(-1).astype(jnp.int32)
    mla_rows, gdn_out, dsa_out = [], [], []
    for i in range(DEPTH):
        kind, j = i % N_MIXERS, i // N_MIXERS
        if kind == 0:
            x, rows = _mla_layer(x, j, cache_mla, pt_flat, w_mla_in, g_mla_q, g_mla_kv, w_mla_uq, w_mla_uk,
                                 w_mla_uv, w_mla_out, ln1_g[i], ln1_b[i], tabs_mla)
            mla_rows.append(rows)
        elif kind == 1:
            x, *st = _gdn_layer(x, j, state_gdn_S, state_gdn_conv, w_gdn_in, w_gdn_conv, gdn_a_log, gdn_dt_bias,
                                g_gdn_out, w_gdn_out, ln1_g[i], ln1_b[i])
            gdn_out.append(st)
        else:
            x, kv, ik = _dsa_layer(x, j, cache_dsa_kv, cache_dsa_idx, pt_flat, w_dsa_in, w_dsa_out,
                                   ln1_g[i], ln1_b[i], tabs_dsa)
            dsa_out.append((kv, ik))
        x = _mlp_ln(x, w_up[i].astype(bf16), w_down[i].astype(bf16), ln2_g[i], ln2_b[i])

    kv_shape = (2, DSA_KV_HEADS, DSA_HD)
    return (
        x[:NP_TOK].reshape(BATCH, SEQ, D_MODEL),
        x[NP_TOK:].reshape(DEC_BATCH, DEC_SEQ, D_MODEL),
        jnp.stack([r[:NP_TOK].reshape(BATCH, SEQ, MLA_CACHE) for r in mla_rows]),
        jnp.stack([r[NP_TOK:].reshape(DEC_BATCH, DEC_SEQ, MLA_CACHE) for r in mla_rows]),
        jnp.stack([st[0] for st in gdn_out]),
        jnp.stack([st[1] for st in gdn_out]),
        jnp.stack([st[2] for st in gdn_out]),
        jnp.stack([st[3] for st in gdn_out]),
        jnp.stack([kv[:NP_TOK].reshape((BATCH, SEQ) + kv_shape) for kv, _ in dsa_out]),
        jnp.stack([ik[:NP_TOK].reshape(BATCH, SEQ, DSA_IDX_DIM) for _, ik in dsa_out]),
        jnp.stack([kv[NP_TOK:].reshape((DEC_BATCH, DEC_SEQ) + kv_shape) for kv, _ in dsa_out]),
        jnp.stack([ik[NP_TOK:].reshape(DEC_BATCH, DEC_SEQ, DSA_IDX_DIM) for _, ik in dsa_out]),
    )
```

```python
import functools
import math

import jax
import jax.numpy as jnp
from jax import lax
from jax.experimental import pallas as pl
from jax.experimental.pallas import tpu as pltpu

f32 = jnp.float32
bf16 = jnp.bfloat16

D_MODEL = 2048
BATCH = 4
SEQ = 2048
DEPTH = 4
DEC_BATCH = 128
DEC_SEQ = 4
PAST_LEN = 8192
PAGE_SIZE = 128
N_PAGES = PAST_LEN // PAGE_SIZE
N_MIXERS = 3

ROPE_THETA = 500000.0
LN_EPS = 1e-5
RMS_EPS = 1e-6
DEEPNORM_ALPHA = (2 * DEPTH) ** 0.25
D_FF = 4 * D_MODEL

MLA_HEADS = 16
MLA_Q_LORA = 512
MLA_KV_LORA = 256
MLA_NOPE = 128
MLA_ROPE = 64
MLA_V = 128
MLA_CACHE = MLA_KV_LORA + MLA_ROPE
MLA_SCALE = (MLA_NOPE + MLA_ROPE) ** -0.5

GDN_HK = 16
GDN_HV = 32
GDN_DK = 128
GDN_DV = 128
GDN_CONV = 4
GDN_CHUNK = 64
GDN_QK = GDN_HK * GDN_DK
GDN_VD = GDN_HV * GDN_DV
GDN_CONV_DIM = 2 * GDN_QK + GDN_VD

DSA_HEADS = 16
DSA_KV_HEADS = 4
DSA_HD = 128
DSA_ROT = DSA_HD // 4
DSA_IDX_HEADS = 16
DSA_IDX_DIM = 128
DSA_TOPK_MAX = 256
DSA_SCALE = DSA_HD ** -0.5
DSA_IW_SCALE = (DSA_IDX_HEADS ** -0.5) * (DSA_IDX_DIM ** -0.5)

NP_TOK = BATCH * SEQ
NS_TOK = DEC_BATCH * DEC_SEQ
NTOK = NP_TOK + NS_TOK

LANES = 128
VMEM_BIG = 56 << 20
NEG_BIG = -1e30
INT_MIN = -(2 ** 31)


def _params(sem, vmem=None):
    return pltpu.CompilerParams(dimension_semantics=sem, vmem_limit_bytes=vmem)


def _nt(a, b):
    return lax.dot_general(a, b, (((1,), (1,)), ((), ())), preferred_element_type=f32)


def _tn(a, b):
    return lax.dot_general(a, b, (((0,), (0,)), ((), ())), preferred_element_type=f32)


def _dot(a, b):
    return jnp.dot(a, b, preferred_element_type=f32)


def _dot_hi(a, b):
    return jnp.dot(a, b, preferred_element_type=f32, precision=lax.Precision.HIGHEST)


def _dot_inv(a, b):
    return _dot(a.astype(bf16), b.astype(bf16))


def _rope_tile(x, c, sa, sb, half):
    return x * c + pltpu.roll(x, LANES - half, 1) * sa + pltpu.roll(x, half, 1) * sb


def _layer_norm(z, g, b):
    mu = jnp.mean(z, -1, keepdims=True)
    zc = z - mu
    var = jnp.mean(zc * zc, -1, keepdims=True)
    return zc * lax.rsqrt(var + LN_EPS) * g + b


def _rope_tables(pos, rot_dim):
    half = rot_dim // 2
    inv = jnp.power(ROPE_THETA, -jnp.arange(half, dtype=f32) * 2.0 / rot_dim)
    ang = pos.astype(f32)[:, None] * inv[None, :]
    cos, sin = jnp.cos(ang), jnp.sin(ang)
    n = pos.shape[0]
    pad = LANES - rot_dim
    c = jnp.concatenate([cos, cos, jnp.ones((n, pad), f32)], 1)
    sa = jnp.concatenate([-sin, jnp.zeros((n, half + pad), f32)], 1)
    sb = jnp.concatenate([jnp.zeros((n, half), f32), sin, jnp.zeros((n, pad), f32)], 1)
    return c, sa, sb


def _mm_kernel(x_ref, w_ref, o_ref, xb_ref):
    @pl.when(pl.program_id(1) == 0)
    def _():
        xb_ref[...] = x_ref[...].astype(bf16)

    o_ref[...] = _dot(xb_ref[...], w_ref[...]).astype(o_ref.dtype)


def _matmul(x, w, *, tm, tn, out_dtype=f32):
    m, k = x.shape
    n = w.shape[1]
    return pl.pallas_call(
        _mm_kernel,
        grid=(m // tm, n // tn),
        in_specs=[pl.BlockSpec((tm, k), lambda i, j: (i, 0)),
                  pl.BlockSpec((k, tn), lambda i, j: (0, j))],
        out_specs=pl.BlockSpec((tm, tn), lambda i, j: (i, j)),
        out_shape=jax.ShapeDtypeStruct((m, n), out_dtype),
        scratch_shapes=[pltpu.VMEM((tm, k), bf16)],
        compiler_params=_params(("parallel", "arbitrary"), VMEM_BIG),
        name="matmul",
    )(x, w)


def _proj_ln_kernel(a_ref, w_ref, x_ref, g_ref, b_ref, y_ref, acc_ref):
    k = pl.program_id(1)

    @pl.when(k == 0)
    def _():
        acc_ref[...] = jnp.zeros_like(acc_ref)

    acc_ref[...] += _dot(a_ref[...], w_ref[...])

    @pl.when(k == pl.num_programs(1) - 1)
    def _():
        z = DEEPNORM_ALPHA * x_ref[...] + acc_ref[...]
        y_ref[...] = _layer_norm(z, g_ref[...], b_ref[...])


def _proj_ln(a, w, x, g, b, *, tm=512, tk=1024):
    m, kd = a.shape
    d = w.shape[1]
    return pl.pallas_call(
        _proj_ln_kernel,
        grid=(m // tm, kd // tk),
        in_specs=[pl.BlockSpec((tm, tk), lambda i, k: (i, k)),
                  pl.BlockSpec((tk, d), lambda i, k: (k, 0)),
                  pl.BlockSpec((tm, d), lambda i, k: (i, 0)),
                  pl.BlockSpec((1, d), lambda i, k: (0, 0)),
                  pl.BlockSpec((1, d), lambda i, k: (0, 0))],
        out_specs=pl.BlockSpec((tm, d), lambda i, k: (i, 0)),
        out_shape=jax.ShapeDtypeStruct((m, d), f32),
        scratch_shapes=[pltpu.VMEM((tm, d), f32)],
        compiler_params=_params(("parallel", "arbitrary"), VMEM_BIG),
        name="proj_ln",
    )(a, w, x, g.reshape(1, d), b.reshape(1, d))


def _mlp_ln_kernel(x_ref, wu_ref, wd_ref, g_ref, b_ref, y_ref, xb_ref, acc_ref):
    j = pl.program_id(1)

    @pl.when(j == 0)
    def _():
        xb_ref[...] = x_ref[...].astype(bf16)
        acc_ref[...] = jnp.zeros_like(acc_ref)

    h = jnp.maximum(_dot(xb_ref[...], wu_ref[...]), 0.0)
    acc_ref[...] += _dot((h * h).astype(bf16), wd_ref[...])

    @pl.when(j == pl.num_programs(1) - 1)
    def _():
        z = DEEPNORM_ALPHA * x_ref[...] + acc_ref[...]
        y_ref[...] = _layer_norm(z, g_ref[...], b_ref[...])


def _mlp_ln(x, w_up, w_down, g, b, *, tm=512, tf=512):
    m, d = x.shape
    ff = w_up.shape[1]
    return pl.pallas_call(
        _mlp_ln_kernel,
        grid=(m // tm, ff // tf),
        in_specs=[pl.BlockSpec((tm, d), lambda i, j: (i, 0)),
                  pl.BlockSpec((d, tf), lambda i, j: (0, j)),
                  pl.BlockSpec((tf, d), lambda i, j: (j, 0)),
                  pl.BlockSpec((1, d), lambda i, j: (0, 0)),
                  pl.BlockSpec((1, d), lambda i, j: (0, 0))],
        out_specs=pl.BlockSpec((tm, d), lambda i, j: (i, 0)),
        out_shape=jax.ShapeDtypeStruct((m, d), f32),
        scratch_shapes=[pltpu.VMEM((tm, d), bf16), pltpu.VMEM((tm, d), f32)],
        compiler_params=_params(("parallel", "arbitrary"), VMEM_BIG),
        name="mlp_ln",
    )(x, w_up, w_down, g.reshape(1, d), b.reshape(1, d))


MLA_IN_PAD = MLA_Q_LORA + MLA_KV_LORA + LANES


def _mla_in_kernel(x_ref, w_ref, gq_ref, gkv_ref, c_ref, sa_ref, sb_ref, cq_ref, rows_ref):
    h = _dot(x_ref[...].astype(bf16), w_ref[...])
    hq = h[:, :MLA_Q_LORA]
    hkv = h[:, MLA_Q_LORA:MLA_Q_LORA + MLA_KV_LORA]
    hr = h[:, MLA_Q_LORA + MLA_KV_LORA:]
    cq = hq * lax.rsqrt(jnp.mean(hq * hq, -1, keepdims=True) + RMS_EPS) * gq_ref[...]
    ckv = hkv * lax.rsqrt(jnp.mean(hkv * hkv, -1, keepdims=True) + RMS_EPS) * gkv_ref[...]
    kr = _rope_tile(hr, c_ref[...], sa_ref[...], sb_ref[...], MLA_ROPE // 2)
    cq_ref[...] = cq.astype(bf16)
    rows_ref[:, :MLA_KV_LORA] = ckv
    rows_ref[:, MLA_KV_LORA:] = kr[:, :MLA_ROPE]


def _mla_in(x, w_pad, g_q, g_kv, tabs, *, tm=512):
    m, d = x.shape
    tab_spec = pl.BlockSpec((tm, LANES), lambda i: (i, 0))
    return pl.pallas_call(
        _mla_in_kernel,
        grid=(m // tm,),
        in_specs=[pl.BlockSpec((tm, d), lambda i: (i, 0)),
                  pl.BlockSpec((d, MLA_IN_PAD), lambda i: (0, 0)),
                  pl.BlockSpec((1, MLA_Q_LORA), lambda i: (0, 0)),
                  pl.BlockSpec((1, MLA_KV_LORA), lambda i: (0, 0)),
                  tab_spec, tab_spec, tab_spec],
        out_specs=[pl.BlockSpec((tm, MLA_Q_LORA), lambda i: (i, 0)),
                   pl.BlockSpec((tm, MLA_CACHE), lambda i: (i, 0))],
        out_shape=[jax.ShapeDtypeStruct((m, MLA_Q_LORA), bf16),
                   jax.ShapeDtypeStruct((m, MLA_CACHE), f32)],
        compiler_params=_params(("parallel",), VMEM_BIG),
        name="mla_in",
    )(x, w_pad, g_q.reshape(1, -1), g_kv.reshape(1, -1), *tabs)


MLA_Q_HEADS = 4


def _mla_q_kernel(cq_ref, wn_ref, wr_ref, wuk_ref, c_ref, sa_ref, sb_ref, q_ref):
    cq = cq_ref[...]
    c, sa, sb = c_ref[...], sa_ref[...], sb_ref[...]
    for h in range(MLA_Q_HEADS):
        qn = _dot(cq, wn_ref[h])
        ql = _dot(qn.astype(bf16), wuk_ref[h])
        qr = _rope_tile(_dot(cq, wr_ref[h]), c, sa, sb, MLA_ROPE // 2)
        q_ref[h, :, :MLA_KV_LORA] = ql.astype(bf16)
        q_ref[h, :, MLA_KV_LORA:] = qr[:, :MLA_ROPE].astype(bf16)


def _mla_q(cq, wn, wr, wuk, tabs, *, tm=512):
    m = cq.shape[0]
    hq = MLA_Q_HEADS
    tab_spec = pl.BlockSpec((tm, LANES), lambda i, h: (i, 0))
    return pl.pallas_call(
        _mla_q_kernel,
        grid=(m // tm, MLA_HEADS // hq),
        in_specs=[pl.BlockSpec((tm, MLA_Q_LORA), lambda i, h: (i, 0)),
                  pl.BlockSpec((hq, MLA_Q_LORA, MLA_NOPE), lambda i, h: (h, 0, 0)),
                  pl.BlockSpec((hq, MLA_Q_LORA, LANES), lambda i, h: (h, 0, 0)),
                  pl.BlockSpec((hq, MLA_NOPE, MLA_KV_LORA), lambda i, h: (h, 0, 0)),
                  tab_spec, tab_spec, tab_spec],
        out_specs=pl.BlockSpec((hq, tm, MLA_CACHE), lambda i, h: (h, i, 0)),
        out_shape=jax.ShapeDtypeStruct((MLA_HEADS, m, MLA_CACHE), bf16),
        compiler_params=_params(("parallel", "arbitrary")),
        name="mla_q",
    )(cq, wn, wr, wuk, *tabs)


MLA_TQ = 128
MLA_TK = 512
MLA_FLASH_GROUPS = 4


def _mla_flash_kernel(q_ref, k_ref, o_ref, m_ref, l_ref, acc_ref):
    qi = pl.program_id(1)
    ki = pl.program_id(2)
    rows = MLA_HEADS * MLA_TQ
    last_k = (qi * MLA_TQ + MLA_TQ - 1) // MLA_TK

    @pl.when(ki == 0)
    def _():
        m_ref[...] = jnp.full_like(m_ref, NEG_BIG)
        l_ref[...] = jnp.zeros_like(l_ref)
        acc_ref[...] = jnp.zeros_like(acc_ref)

    def update(masked):
        k = k_ref[...]
        grows = rows // MLA_FLASH_GROUPS
        gheads = MLA_HEADS // MLA_FLASH_GROUPS
        if masked:
            qpos = qi * MLA_TQ + (lax.broadcasted_iota(jnp.int32, (grows, MLA_TK), 0) & (MLA_TQ - 1))
            kpos = ki * MLA_TK + lax.broadcasted_iota(jnp.int32, (grows, MLA_TK), 1)
            visible = kpos <= qpos
        for g in range(MLA_FLASH_GROUPS):
            r = slice(g * grows, (g + 1) * grows)
            q = q_ref[g * gheads:(g + 1) * gheads].reshape(grows, MLA_CACHE)
            s = _nt(q, k) * MLA_SCALE
            if masked:
                s = jnp.where(visible, s, NEG_BIG)
            m_prev = m_ref[r]
            m_new = jnp.maximum(m_prev, jnp.max(s, -1, keepdims=True))
            alpha = jnp.exp(m_prev - m_new)
            p = jnp.exp(s - m_new)
            l_ref[r] = alpha * l_ref[r] + jnp.sum(p, -1, keepdims=True)
            acc_ref[r] = alpha * acc_ref[r] + _dot(p.astype(bf16), k[:, :MLA_KV_LORA])
            m_ref[r] = m_new

    pl.when(ki < last_k)(functools.partial(update, False))
    pl.when(ki == last_k)(functools.partial(update, True))

    @pl.when(ki == pl.num_programs(2) - 1)
    def _():
        o = acc_ref[...] / l_ref[...]
        o_ref[...] = o.astype(bf16).reshape(MLA_HEADS, MLA_TQ, MLA_KV_LORA)


def _mla_flash(q_cat, rows_bf):
    nq = SEQ // MLA_TQ
    nk = SEQ // MLA_TK

    def k_map(b, qi, ki):
        return (b * nk + jnp.minimum(ki, (qi * MLA_TQ + MLA_TQ - 1) // MLA_TK), 0)

    return pl.pallas_call(
        _mla_flash_kernel,
        grid=(BATCH, nq, nk),
        in_specs=[pl.BlockSpec((MLA_HEADS, MLA_TQ, MLA_CACHE), lambda b, qi, ki: (0, b * nq + qi, 0)),
                  pl.BlockSpec((MLA_TK, MLA_CACHE), k_map)],
        out_specs=pl.BlockSpec((MLA_HEADS, MLA_TQ, MLA_KV_LORA), lambda b, qi, ki: (0, b * nq + qi, 0)),
        out_shape=jax.ShapeDtypeStruct((MLA_HEADS, NP_TOK, MLA_KV_LORA), bf16),
        scratch_shapes=[pltpu.VMEM((MLA_HEADS * MLA_TQ, 1), f32),
                        pltpu.VMEM((MLA_HEADS * MLA_TQ, 1), f32),
                        pltpu.VMEM((MLA_HEADS * MLA_TQ, MLA_KV_LORA), f32)],
        compiler_params=_params(("parallel", "parallel", "arbitrary"), VMEM_BIG),
        name="mla_flash",
    )(q_cat, rows_bf)


def _mla_uv_kernel(o_ref, w_ref, v_ref):
    for h in range(MLA_HEADS):
        v_ref[:, h * MLA_V:(h + 1) * MLA_V] = _dot(o_ref[h], w_ref[h]).astype(bf16)


def _mla_uv(o_lat, wuv, *, tm=512):
    m = o_lat.shape[1]
    return pl.pallas_call(
        _mla_uv_kernel,
        grid=(m // tm,),
        in_specs=[pl.BlockSpec((MLA_HEADS, tm, MLA_KV_LORA), lambda i: (0, i, 0)),
                  pl.BlockSpec((MLA_HEADS, MLA_KV_LORA, MLA_V), lambda i: (0, 0, 0))],
        out_specs=pl.BlockSpec((tm, MLA_HEADS * MLA_V), lambda i: (i, 0)),
        out_shape=jax.ShapeDtypeStruct((m, MLA_HEADS * MLA_V), bf16),
        compiler_params=_params(("parallel",), VMEM_BIG),
        name="mla_uv",
    )(o_lat, wuv)


def _paged_fetch(pt_ref, page_src, buf_ref, sem_ref, pages):
    n_chunks = pl.num_programs(1)
    step = pl.program_id(0) * n_chunks + pl.program_id(1)
    total = pl.num_programs(0) * n_chunks
    slot = lax.rem(step, 2)

    def copies(s, sl):
        return [pltpu.make_async_copy(page_src(pt_ref[s * pages + p]), buf_ref.at[sl, p], sem_ref.at[sl])
                for p in range(pages)]

    @pl.when(step == 0)
    def _():
        for cp in copies(0, 0):
            cp.start()

    @pl.when(step + 1 < total)
    def _():
        for cp in copies(step + 1, 1 - slot):
            cp.start()

    for cp in copies(step, slot):
        cp.wait()
    return slot


MLA_PG = 32
MLA_SROWS = MLA_HEADS * DEC_SEQ
NEW_PAD = LANES


def _mla_paged_kernel(pt_ref, q_ref, new_ref, cache_ref, o_ref, buf_ref, sem_ref, m_ref, l_ref, acc_ref,
                      *, layer):
    c = pl.program_id(1)
    slot = _paged_fetch(pt_ref, lambda pg: cache_ref.at[layer, pg], buf_ref, sem_ref, MLA_PG)
    q = q_ref[0]

    @pl.when(c == 0)
    def _():
        kn = new_ref[0].astype(bf16)
        s = _nt(q, kn) * MLA_SCALE
        tq = lax.broadcasted_iota(jnp.int32, s.shape, 0) & (DEC_SEQ - 1)
        tk = lax.broadcasted_iota(jnp.int32, s.shape, 1)
        s = jnp.where(tk <= tq, s, NEG_BIG)
        m0 = jnp.max(s, -1, keepdims=True)
        p = jnp.exp(s - m0)
        m_ref[...] = m0
        l_ref[...] = jnp.sum(p, -1, keepdims=True)
        acc_ref[...] = _dot(p.astype(bf16), kn[:, :MLA_KV_LORA])

    kts = [buf_ref[slot, pg].astype(bf16) for pg in range(MLA_PG)]
    s = jnp.concatenate([_dot(q, kt) for kt in kts], axis=1) * MLA_SCALE
    m_prev = m_ref[...]
    m_new = jnp.maximum(m_prev, jnp.max(s, -1, keepdims=True))
    alpha = jnp.exp(m_prev - m_new)
    p = jnp.exp(s - m_new)
    l_ref[...] = alpha * l_ref[...] + jnp.sum(p, -1, keepdims=True)
    pb = p.astype(bf16)
    pv = _nt(pb[:, :PAGE_SIZE], kts[0][:MLA_KV_LORA])
    for pg in range(1, MLA_PG):
        pv = pv + _nt(pb[:, pg * PAGE_SIZE:(pg + 1) * PAGE_SIZE], kts[pg][:MLA_KV_LORA])
    acc_ref[...] = alpha * acc_ref[...] + pv
    m_ref[...] = m_new

    @pl.when(c == pl.num_programs(1) - 1)
    def _():
        o_ref[0] = (acc_ref[...] / l_ref[...]).astype(bf16)


def _mla_paged(pt_flat, q_s, rows_new, cache, layer):
    nc = N_PAGES // MLA_PG
    kern = functools.partial(_mla_paged_kernel, layer=layer)
    return pl.pallas_call(
        kern,
        grid_spec=pltpu.PrefetchScalarGridSpec(
            num_scalar_prefetch=1,
            grid=(DEC_BATCH, nc),
            in_specs=[pl.BlockSpec((1, MLA_SROWS, MLA_CACHE), lambda b, c, pt: (b, 0, 0)),
                      pl.BlockSpec((1, NEW_PAD, MLA_CACHE), lambda b, c, pt: (b, 0, 0)),
                      pl.BlockSpec(memory_space=pl.ANY)],
            out_specs=pl.BlockSpec((1, MLA_SROWS, MLA_KV_LORA), lambda b, c, pt: (b, 0, 0)),
            scratch_shapes=[pltpu.VMEM((2, MLA_PG, MLA_CACHE, PAGE_SIZE), f32),
                            pltpu.SemaphoreType.DMA((2,)),
                            pltpu.VMEM((MLA_SROWS, 1), f32),
                            pltpu.VMEM((MLA_SROWS, 1), f32),
                            pltpu.VMEM((MLA_SROWS, MLA_KV_LORA), f32)]),
        out_shape=jax.ShapeDtypeStruct((DEC_BATCH, MLA_SROWS, MLA_KV_LORA), bf16),
        compiler_params=_params(("arbitrary", "arbitrary"), VMEM_BIG),
        name="mla_paged",
    )(pt_flat, q_s, rows_new, cache)


def _sortable(x):
    b = pltpu.bitcast(x, jnp.int32)
    return b ^ ((b >> 31) & 0x7FFFFFFF)


def _count(mask):
    return jnp.sum(jnp.where(mask, 1.0, 0.0), -1, keepdims=True)


def _topk_mask(si, col, kk, idx_bits, two_bit_rounds=False):
    kf = float(kk)
    tau0 = jnp.where(_count(si >= 0) >= kf, 0, INT_MIN).astype(jnp.int32)

    def tau_step(i, tau):
        cand = tau | jnp.left_shift(jnp.int32(1), 30 - i)
        return jnp.where(_count(si >= cand) >= kf, cand, tau)

    def tau_step2(i, tau):
        hi = jnp.left_shift(jnp.int32(1), 30 - 2 * i)
        lo = jnp.left_shift(jnp.int32(1), 29 - 2 * i)
        c1, c2, c3 = tau | lo, tau | hi, tau | hi | lo
        ok1, ok2, ok3 = [_count(si >= c) >= kf for c in (c1, c2, c3)]
        return jnp.where(ok3, c3, jnp.where(ok2, c2, jnp.where(ok1, c1, tau)))

    if two_bit_rounds:
        tau = tau_step(30, lax.fori_loop(0, 15, tau_step2, tau0))
    else:
        tau = lax.fori_loop(0, 31, tau_step, tau0)
    gt = si > tau
    eq = si == tau
    need = kf - _count(gt)
    surplus = jnp.max(_count(eq) - need)

    def cut_search(_):
        def cut_step(i, cut):
            cand = cut | jnp.left_shift(jnp.int32(1), idx_bits - 1 - i)
            return jnp.where(_count(eq & (col < cand)) < need, cand, cut)
        return lax.fori_loop(0, idx_bits, cut_step, jnp.zeros_like(tau))

    def no_cut(_):
        return jnp.full_like(tau, (1 << idx_bits) - 1)

    cut = lax.cond(surplus > 0.0, cut_search, no_cut, 0)
    return gt | (eq & (col <= cut))


DSA_W = 5376
DSA_COL_IQ = 0
DSA_COL_Q = 2048
DSA_COL_KV = 4096
DSA_COL_IK = 5120
DSA_COL_IW = 5248
DSA_TN = 768
DSA_TQ = 128
DSA_KBUCKET = 512


def _dsa_in_kernel(x_ref, w_ref, c_ref, sa_ref, sb_ref, o_ref, xb_ref):
    j = pl.program_id(1)

    @pl.when(j == 0)
    def _():
        xb_ref[...] = x_ref[...].astype(bf16)

    h = _dot(xb_ref[...], w_ref[...])
    c, sa, sb = c_ref[...], sa_ref[...], sb_ref[...]
    for s in range(DSA_TN // LANES):
        tile = jnp.full((h.shape[0], LANES), j * (DSA_TN // LANES) + s, jnp.int32)
        x = h[:, s * LANES:(s + 1) * LANES]
        roped = _rope_tile(x, c, sa, sb, DSA_ROT // 2)
        is_v = (tile >= (DSA_COL_KV + 512) // LANES) & (tile < DSA_COL_IK // LANES)
        is_w = tile == DSA_COL_IW // LANES
        y = jnp.where(is_v, x, roped)
        y = jnp.where(is_w, x * DSA_IW_SCALE, y)
        o_ref[:, s * LANES:(s + 1) * LANES] = y


def _dsa_in(x, w, tabs, *, tm=512):
    m, d = x.shape
    tab_spec = pl.BlockSpec((tm, LANES), lambda i, j: (i, 0))
    return pl.pallas_call(
        _dsa_in_kernel,
        grid=(m // tm, DSA_W // DSA_TN),
        in_specs=[pl.BlockSpec((tm, d), lambda i, j: (i, 0)),
                  pl.BlockSpec((d, DSA_TN), lambda i, j: (0, j)),
                  tab_spec, tab_spec, tab_spec],
        out_specs=pl.BlockSpec((tm, DSA_TN), lambda i, j: (i, j)),
        out_shape=jax.ShapeDtypeStruct((m, DSA_W), f32),
        scratch_shapes=[pltpu.VMEM((tm, d), bf16)],
        compiler_params=_params(("parallel", "arbitrary"), VMEM_BIG),
        name="dsa_in",
    )(x, w, *tabs)


def _dsa_prompt_kernel(iq_ref, q_ref, iw_ref, ik_ref, kv_ref, o_ref, ikb_ref, kvb_ref):
    qi = pl.program_id(1)

    @pl.when(qi == 0)
    def _():
        ikb_ref[...] = ik_ref[...].astype(bf16)
        kvb_ref[...] = kv_ref[...].astype(bf16)

    def attend(klen):
        ikb = ikb_ref[:klen, :]
        iw = iw_ref[...]
        sc = jnp.zeros((DSA_TQ, klen), f32)
        for h in range(DSA_IDX_HEADS):
            s = _nt(iq_ref[:, h * LANES:(h + 1) * LANES].astype(bf16), ikb)
            sc = sc + iw[:, h:h + 1] * jnp.maximum(s, 0.0)

        row = lax.broadcasted_iota(jnp.int32, (DSA_TQ, klen), 0)
        col = lax.broadcasted_iota(jnp.int32, (DSA_TQ, klen), 1)
        causal = col <= qi * DSA_TQ + row
        si = jnp.where(causal, _sortable(sc), INT_MIN)
        sel = _topk_mask(si, col, DSA_TOPK_MAX, 11) & causal
        bias = jnp.where(sel, 0.0, NEG_BIG)[None]

        group = DSA_HEADS // DSA_KV_HEADS
        for j in range(DSA_KV_HEADS):
            kj = kvb_ref[:klen, j * DSA_HD:(j + 1) * DSA_HD]
            vj = kvb_ref[:klen, (DSA_KV_HEADS + j) * DSA_HD:(DSA_KV_HEADS + j + 1) * DSA_HD]
            qj = jnp.concatenate([q_ref[:, h * DSA_HD:(h + 1) * DSA_HD]
                                  for h in range(j * group, (j + 1) * group)], axis=0).astype(bf16)
            s = (_nt(qj, kj) * DSA_SCALE).reshape(group, DSA_TQ, klen) + bias
            s = s.reshape(group * DSA_TQ, klen)
            p = jnp.exp(s - jnp.max(s, -1, keepdims=True))
            o = _dot(p.astype(bf16), vj) / jnp.sum(p, -1, keepdims=True)
            for g in range(group):
                h = j * group + g
                o_ref[:, h * DSA_HD:(h + 1) * DSA_HD] = o[g * DSA_TQ:(g + 1) * DSA_TQ].astype(bf16)

    n_buckets = SEQ // DSA_KBUCKET
    bucket = (qi * DSA_TQ) // DSA_KBUCKET
    for k in range(n_buckets):
        pl.when(bucket == k)(functools.partial(attend, (k + 1) * DSA_KBUCKET))


def _dsa_prompt(hd):
    nq = SEQ // DSA_TQ
    return pl.pallas_call(
        _dsa_prompt_kernel,
        grid=(BATCH, nq),
        in_specs=[pl.BlockSpec((DSA_TQ, 2048), lambda b, qi: (b * nq + qi, DSA_COL_IQ // 2048)),
                  pl.BlockSpec((DSA_TQ, 2048), lambda b, qi: (b * nq + qi, DSA_COL_Q // 2048)),
                  pl.BlockSpec((DSA_TQ, LANES), lambda b, qi: (b * nq + qi, DSA_COL_IW // LANES)),
                  pl.BlockSpec((SEQ, LANES), lambda b, qi: (b, DSA_COL_IK // LANES)),
                  pl.BlockSpec((SEQ, 1024), lambda b, qi: (b, DSA_COL_KV // 1024))],
        out_specs=pl.BlockSpec((DSA_TQ, DSA_HEADS * DSA_HD), lambda b, qi: (b * nq + qi, 0)),
        out_shape=jax.ShapeDtypeStruct((NP_TOK, DSA_HEADS * DSA_HD), bf16),
        scratch_shapes=[pltpu.VMEM((SEQ, LANES), bf16), pltpu.VMEM((SEQ, 1024), bf16)],
        compiler_params=_params(("parallel", "arbitrary"), VMEM_BIG),
        name="dsa_prompt",
    )(hd, hd, hd, hd, hd)


DSI_PG = 32
DSI_W = PAST_LEN + LANES
DSI_ROWS = 2 * DEC_SEQ


def _dsa_idx_kernel(pt_ref, iq_ref, iw_ref, ikn_ref, cache_ref, mask_ref, buf_ref, sem_ref, sc_ref, *, layer):
    c = pl.program_id(1)
    slot = _paged_fetch(pt_ref, lambda pg: cache_ref.at[layer, pg], buf_ref, sem_ref, DSI_PG)
    iq = iq_ref[0]
    iw = iw_ref[0]
    n = DSI_PG * PAGE_SIZE
    k = buf_ref[slot].reshape(n, DSA_IDX_DIM).astype(bf16)
    s = jnp.maximum(_nt(iq, k), 0.0) * iw
    sc_ref[:, pl.ds(pl.multiple_of(c * n, LANES), n)] = jnp.sum(s.reshape(DSI_ROWS, DSA_IDX_HEADS, n), axis=1)

    @pl.when(c == pl.num_programs(1) - 1)
    def _():
        sn = jnp.maximum(_nt(iq, ikn_ref[0].astype(bf16)), 0.0) * iw
        sc_ref[:, PAST_LEN:] = jnp.sum(sn.reshape(DSI_ROWS, DSA_IDX_HEADS, LANES), axis=1)
        sc = sc_ref[...]
        t = lax.broadcasted_iota(jnp.int32, sc.shape, 0) & (DEC_SEQ - 1)
        col = lax.broadcasted_iota(jnp.int32, sc.shape, 1)
        valid = col <= PAST_LEN + t
        si = jnp.where(valid, _sortable(sc), INT_MIN)
        sel = _topk_mask(si, col, DSA_TOPK_MAX, 14, two_bit_rounds=True) & valid
        mask_ref[0] = jnp.where(sel, 1.0, 0.0)


def _dsa_idx(pt_flat, iq_s, iw_s, ik_new, cache, layer):
    nc = N_PAGES // DSI_PG
    kern = functools.partial(_dsa_idx_kernel, layer=layer)
    rows = DSI_ROWS * DSA_IDX_HEADS
    return pl.pallas_call(
        kern,
        grid_spec=pltpu.PrefetchScalarGridSpec(
            num_scalar_prefetch=1,
            grid=(DEC_BATCH, nc),
            in_specs=[pl.BlockSpec((1, rows, DSA_IDX_DIM), lambda b, c, pt: (b, 0, 0)),
                      pl.BlockSpec((1, rows, 1), lambda b, c, pt: (b, 0, 0)),
                      pl.BlockSpec((1, LANES, DSA_IDX_DIM), lambda b, c, pt: (b, 0, 0)),
                      pl.BlockSpec(memory_space=pl.ANY)],
            out_specs=pl.BlockSpec((1, DSI_ROWS, DSI_W), lambda b, c, pt: (b, 0, 0)),
            scratch_shapes=[pltpu.VMEM((2, DSI_PG, PAGE_SIZE, DSA_IDX_DIM), f32),
                            pltpu.SemaphoreType.DMA((2,)),
                            pltpu.VMEM((DSI_ROWS, DSI_W), f32)]),
        out_shape=jax.ShapeDtypeStruct((DEC_BATCH, DSI_ROWS, DSI_W), f32),
        compiler_params=_params(("arbitrary", "arbitrary"), VMEM_BIG),
        name="dsa_idx",
    )(pt_flat, iq_s, iw_s, ik_new, cache)


DSP_PG = 8
DSP_ROWS = DSA_HEADS * DEC_SEQ
KV_SLICES = 2 * DSA_KV_HEADS
PAGE_ROWS = PAGE_SIZE * KV_SLICES


def _dsa_paged_kernel(pt_ref, q_ref, mask_ref, kvn_ref, cache_ref, o_ref, buf_ref, sem_ref,
                      m_ref, l_ref, acc_ref, *, layer):
    c = pl.program_id(1)
    slot = _paged_fetch(pt_ref, lambda pg: cache_ref.at[layer, pg], buf_ref, sem_ref, DSP_PG)
    q = q_ref[0]
    shape = (DSP_ROWS, PAGE_ROWS)
    row = lax.broadcasted_iota(jnp.int32, shape, 0)
    lane = lax.broadcasted_iota(jnp.int32, shape, 1)
    head_ok = (lane & (KV_SLICES - 1)) == (row >> 4)
    er = lax.broadcasted_iota(jnp.int32, (PAGE_SIZE, PAGE_ROWS), 0)
    el = lax.broadcasted_iota(jnp.int32, (PAGE_SIZE, PAGE_ROWS), 1)
    expand = jnp.where((el >> 3) == er, 1.0, 0.0).astype(bf16)

    def attend(xs, sel, m_prev, l_prev, acc_prev):
        n = len(xs)
        sel_rows = jnp.concatenate([sel[:, i * PAGE_SIZE:(i + 1) * PAGE_SIZE] for i in range(n)], axis=0)
        e = _dot(sel_rows.astype(bf16), expand)
        reps = DSP_ROWS // DSI_ROWS
        ss = []
        for i in range(n):
            ei = jnp.concatenate([e[i * DSI_ROWS:(i + 1) * DSI_ROWS]] * reps, axis=0)
            ss.append(jnp.where(head_ok & (ei > 0.5), _nt(q, xs[i]) * DSA_SCALE, NEG_BIG))
        m_blk = ss[0].max(-1, keepdims=True)
        for i in range(1, n):
            m_blk = jnp.maximum(m_blk, ss[i].max(-1, keepdims=True))
        m_new = jnp.maximum(m_prev, m_blk)
        alpha = jnp.exp(m_prev - m_new)
        m_exp = jnp.maximum(m_new, 0.1 * NEG_BIG)
        l_new = alpha * l_prev
        acc = alpha * acc_prev
        for i in range(n):
            p = jnp.exp(ss[i] - m_exp)
            l_new = l_new + jnp.sum(p, -1, keepdims=True)
            acc = acc + _dot(pltpu.roll(p, DSA_KV_HEADS, 1).astype(bf16), xs[i])
        return m_new, l_new, acc

    @pl.when(c == 0)
    def _():
        m0 = jnp.full((DSP_ROWS, 1), NEG_BIG, f32)
        l0 = jnp.zeros((DSP_ROWS, 1), f32)
        a0 = jnp.zeros((DSP_ROWS, DSA_HD), f32)
        m1, l1, a1 = attend([kvn_ref[0].astype(bf16)], mask_ref[0, :, PAST_LEN:], m0, l0, a0)
        m_ref[...] = m1
        l_ref[...] = l1
        acc_ref[...] = a1

    off = pl.multiple_of(c * (DSP_PG * PAGE_SIZE), LANES)
    pages = [buf_ref[slot, i].astype(bf16) for i in range(DSP_PG)]
    m1, l1, a1 = attend(pages, mask_ref[0, :, pl.ds(off, DSP_PG * PAGE_SIZE)], m_ref[...], l_ref[...], acc_ref[...])
    m_ref[...] = m1
    l_ref[...] = l1
    acc_ref[...] = a1

    @pl.when(c == pl.num_programs(1) - 1)
    def _():
        o_ref[0] = (a1 / l1).astype(bf16)


def _dsa_paged(pt_flat, q_s, mask, kv_new, cache, layer):
    nc = N_PAGES // DSP_PG
    kern = functools.partial(_dsa_paged_kernel, layer=layer)
    return pl.pallas_call(
        kern,
        grid_spec=pltpu.PrefetchScalarGridSpec(
            num_scalar_prefetch=1,
            grid=(DEC_BATCH, nc),
            in_specs=[pl.BlockSpec((1, DSP_ROWS, DSA_HD), lambda b, c, pt: (b, 0, 0)),
                      pl.BlockSpec((1, DSI_ROWS, DSI_W), lambda b, c, pt: (b, 0, 0)),
                      pl.BlockSpec((1, PAGE_ROWS, DSA_HD), lambda b, c, pt: (b, 0, 0)),
                      pl.BlockSpec(memory_space=pl.ANY)],
            out_specs=pl.BlockSpec((1, DSP_ROWS, DSA_HD), lambda b, c, pt: (b, 0, 0)),
            scratch_shapes=[pltpu.VMEM((2, DSP_PG, PAGE_ROWS, DSA_HD), f32),
                            pltpu.SemaphoreType.DMA((2,)),
                            pltpu.VMEM((DSP_ROWS, 1), f32),
                            pltpu.VMEM((DSP_ROWS, 1), f32),
                            pltpu.VMEM((DSP_ROWS, DSA_HD), f32)]),
        out_shape=jax.ShapeDtypeStruct((DEC_BATCH, DSP_ROWS, DSA_HD), bf16),
        compiler_params=_params(("arbitrary", "arbitrary"), VMEM_BIG),
        name="dsa_paged",
    )(pt_flat, q_s, mask, kv_new, cache)


GDN_MAIN = GDN_CONV_DIM + GDN_VD
GDN_HG = 8
GDN_NG = GDN_HV // GDN_HG
CONV_TC = 512


def _sigmoid(x):
    return 1.0 / (1.0 + jnp.exp(-x))


def _gdn_ba_kernel(x_ref, w_ref, alog_ref, dtb_ref, o_ref):
    h = _dot(x_ref[...].astype(bf16), w_ref[...])
    a = h + dtb_ref[...]
    softplus = jnp.maximum(a, 0.0) + jnp.log1p(jnp.exp(-jnp.abs(a)))
    g = -jnp.exp(alog_ref[...]) * softplus
    lane = lax.broadcasted_iota(jnp.int32, h.shape, 1)
    o_ref[...] = jnp.where(lane < GDN_HV, _sigmoid(h), g)


def _gdn_ba(x, w_ba, a_log, dt_bias, *, tm=512):
    m, d = x.shape
    pad = jnp.zeros((GDN_HV,), f32)
    zeros = jnp.zeros((LANES - 2 * GDN_HV,), f32)
    alog = jnp.concatenate([pad, a_log.astype(f32), zeros]).reshape(1, LANES)
    dtb = jnp.concatenate([pad, dt_bias.astype(f32), zeros]).reshape(1, LANES)
    return pl.pallas_call(
        _gdn_ba_kernel,
        grid=(m // tm,),
        in_specs=[pl.BlockSpec((tm, d), lambda i: (i, 0)),
                  pl.BlockSpec((d, LANES), lambda i: (0, 0)),
                  pl.BlockSpec((1, LANES), lambda i: (0, 0)),
                  pl.BlockSpec((1, LANES), lambda i: (0, 0))],
        out_specs=pl.BlockSpec((tm, LANES), lambda i: (i, 0)),
        out_shape=jax.ShapeDtypeStruct((m, LANES), f32),
        compiler_params=_params(("parallel",)),
        name="gdn_ba",
    )(x, w_ba, alog, dtb)


def _conv_finish(acc, ci, o_ref):
    y = acc * _sigmoid(acc)
    heads_per_tile = CONV_TC // GDN_DK
    q_tiles = GDN_QK // CONV_TC
    for s in range(heads_per_tile):
        seg = y[:, s * GDN_DK:(s + 1) * GDN_DK]
        n = seg * lax.rsqrt(jnp.sum(seg * seg, -1, keepdims=True) + RMS_EPS)
        tile = jnp.full(seg.shape, ci, jnp.int32)
        n = jnp.where(tile < q_tiles, n * (GDN_DK ** -0.5), n)
        o_ref[:, s * GDN_DK:(s + 1) * GDN_DK] = jnp.where(tile < 2 * q_tiles, n, seg)


def _conv_prompt_kernel(x_ref, halo_ref, w_ref, o_ref, ext_ref, *, tt):
    ti = pl.program_id(1)
    ci = pl.program_id(2)
    first = jnp.full(halo_ref.shape, ti, jnp.int32) == 0
    ext_ref[0:8, :] = jnp.where(first, 0.0, halo_ref[...])
    ext_ref[8:, :] = x_ref[...]
    w = w_ref[...]
    acc = ext_ref[pl.ds(8 - (GDN_CONV - 1), tt), :] * w[0:1, :]
    for j in range(1, GDN_CONV):
        acc = acc + ext_ref[pl.ds(8 - (GDN_CONV - 1) + j, tt), :] * w[j:j + 1, :]
    _conv_finish(acc, ci, o_ref)


def _conv_prompt(hg, w_conv_t, *, tt=512):
    nt = SEQ // tt
    kern = functools.partial(_conv_prompt_kernel, tt=tt)

    def halo_map(b, ti, ci):
        return (jnp.maximum(b * (SEQ // 8) + ti * (tt // 8) - 1, 0), ci)

    return pl.pallas_call(
        kern,
        grid=(BATCH, nt, GDN_CONV_DIM // CONV_TC),
        in_specs=[pl.BlockSpec((tt, CONV_TC), lambda b, ti, ci: (b * nt + ti, ci)),
                  pl.BlockSpec((8, CONV_TC), halo_map),
                  pl.BlockSpec((GDN_CONV, CONV_TC), lambda b, ti, ci: (0, ci))],
        out_specs=pl.BlockSpec((tt, CONV_TC), lambda b, ti, ci: (b * nt + ti, ci)),
        out_shape=jax.ShapeDtypeStruct((NP_TOK, GDN_CONV_DIM), f32),
        scratch_shapes=[pltpu.VMEM((tt + 8, CONV_TC), f32)],
        compiler_params=_params(("parallel", "parallel", "parallel")),
        name="conv_prompt",
    )(hg, hg, w_conv_t)


def _conv_sample_kernel(x_ref, st_ref, w_ref, o_ref):
    ci = pl.program_id(0)
    w = w_ref[...]
    xp = [st_ref[i] for i in range(GDN_CONV - 1)] + [x_ref[i] for i in range(DEC_SEQ)]
    for t in range(DEC_SEQ):
        acc = xp[t] * w[0:1, :]
        for j in range(1, GDN_CONV):
            acc = acc + xp[t + j] * w[j:j + 1, :]
        _conv_finish(acc, ci, o_ref.at[t])


def _conv_sample(x_t, st_t, w_conv_t):
    return pl.pallas_call(
        _conv_sample_kernel,
        grid=(GDN_CONV_DIM // CONV_TC,),
        in_specs=[pl.BlockSpec((DEC_SEQ, DEC_BATCH, CONV_TC), lambda ci: (0, 0, ci)),
                  pl.BlockSpec((GDN_CONV - 1, DEC_BATCH, CONV_TC), lambda ci: (0, 0, ci)),
                  pl.BlockSpec((GDN_CONV, CONV_TC), lambda ci: (0, ci))],
        out_specs=pl.BlockSpec((DEC_SEQ, DEC_BATCH, CONV_TC), lambda ci: (0, 0, ci)),
        out_shape=jax.ShapeDtypeStruct((DEC_SEQ, DEC_BATCH, GDN_CONV_DIM), f32),
        compiler_params=_params(("parallel",)),
        name="conv_sample",
    )(x_t, st_t, w_conv_t)


def _gdn_chunk_kernel(*refs, chunk, n_chunks, has_s0):
    if has_s0:
        q_ref, k_ref, v_ref, z_ref, gb_ref, gt_ref, gout_ref, s0_ref, o_ref, sfin_ref, s_ref = refs
    else:
        q_ref, k_ref, v_ref, z_ref, gb_ref, gt_ref, gout_ref, o_ref, sfin_ref, s_ref = refs
    tb = pl.program_id(2)

    @pl.when(tb == 0)
    def _():
        if has_s0:
            s_ref[...] = s0_ref[0]
        else:
            s_ref[...] = jnp.zeros_like(s_ref)

    ii = lax.broadcasted_iota(jnp.int32, (chunk, chunk), 0)
    jj = lax.broadcasted_iota(jnp.int32, (chunk, chunk), 1)
    lower = ii >= jj
    strict = ii > jj
    tri = jnp.where(lower, 1.0, 0.0)
    tri_t = jnp.where(ii <= jj, 1.0, 0.0)
    eye = jnp.where(ii == jj, 1.0, 0.0)
    gout = gout_ref[...]
    levels = chunk.bit_length() - 1

    def chunk_step(ci, carry):
        r0 = pl.multiple_of(ci * chunk, chunk)
        gb = gb_ref[0, ci]
        gcum_all = _dot_hi(tri, gb)
        gcum_t = _dot_hi(gt_ref[0, ci], tri_t)
        heads = range(GDN_HG)
        rows = pl.ds(r0, chunk)
        hks = [i // (GDN_HV // GDN_HK) for i in heads]
        qh = [q_ref[rows, hk * GDN_DK:(hk + 1) * GDN_DK] for hk in hks]
        kh = [k_ref[rows, hk * GDN_DK:(hk + 1) * GDN_DK] for hk in hks]
        vh = [v_ref[rows, i * GDN_DV:(i + 1) * GDN_DV] for i in heads]
        gc = [gcum_all[:, i:i + 1] for i in heads]
        bc = [gb[:, GDN_HG + i:GDN_HG + i + 1] for i in heads]
        decay = [jnp.where(lower, jnp.exp(jnp.where(lower, gc[i] - gcum_t[i:i + 1, :], 0.0)), 0.0) for i in heads]
        egc = [jnp.exp(g) for g in gc]
        kb = [kh[i] * bc[i] for i in heads]
        khb = [k.astype(bf16) for k in kh]
        lmat = [jnp.where(strict, _nt(kb[i].astype(bf16), khb[i]) * decay[i], 0.0) for i in heads]
        a_intra = [jnp.where(lower, _nt(qh[i].astype(bf16), khb[i]) * decay[i], 0.0).astype(bf16) for i in heads]
        rhs = [jnp.concatenate([vh[i] * bc[i], kb[i] * egc[i]], axis=1) for i in heads]
        tinv = [eye - m for m in lmat]
        lp = lmat
        for _ in range(levels - 1):
            lp = [_dot_inv(m, m) for m in lp]
            tinv = [t + _dot_inv(t, m) for t, m in zip(tinv, lp)]
        sol = [_dot_inv(tinv[i], rhs[i]) for i in heads]
        s_old = [s_ref[i] for i in heads]
        sb = [s.astype(bf16) for s in s_old]
        ws = [_dot(sol[i][:, GDN_DV:].astype(bf16), sb[i]) for i in heads]
        qs = [_dot((qh[i] * egc[i]).astype(bf16), sb[i]) for i in heads]
        vnb = [(sol[i][:, :GDN_DV] - ws[i]).astype(bf16) for i in heads]
        o = [qs[i] + _dot(a_intra[i], vnb[i]) for i in heads]
        for i in heads:
            g_last = gc[i][chunk - 1:chunk, :]
            kdec = (kh[i] * jnp.exp(g_last - gc[i])).astype(bf16)
            s_ref[i] = s_old[i] * jnp.exp(g_last) + _tn(kdec, vnb[i])
        for i in heads:
            on = o[i] * lax.rsqrt(jnp.mean(o[i] * o[i], -1, keepdims=True) + RMS_EPS) * gout
            zh = z_ref[rows, i * GDN_DV:(i + 1) * GDN_DV]
            o_ref[rows, i * GDN_DV:(i + 1) * GDN_DV] = (on * (zh * _sigmoid(zh))).astype(bf16)
        return carry

    lax.fori_loop(0, n_chunks, chunk_step, 0)

    @pl.when(tb == pl.num_programs(2) - 1)
    def _():
        sfin_ref[0] = s_ref[...]


def _gdn_chunks(qkv, z, z_col0, gb, gt, g_out, s0, *, n_seq, seq_len, chunk, block):
    rows = n_seq * seq_len
    nb = seq_len // block
    n_chunks = block // chunk
    has_s0 = s0 is not None
    qk_w = GDN_HG // (GDN_HV // GDN_HK) * GDN_DK
    v_w = GDN_HG * GDN_DV
    kern = functools.partial(_gdn_chunk_kernel, chunk=chunk, n_chunks=n_chunks, has_s0=has_s0)
    in_specs = [pl.BlockSpec((block, qk_w), lambda s, g, t: (s * nb + t, g)),
                pl.BlockSpec((block, qk_w), lambda s, g, t: (s * nb + t, GDN_QK // qk_w + g)),
                pl.BlockSpec((block, v_w), lambda s, g, t: (s * nb + t, 2 * GDN_QK // v_w + g)),
                pl.BlockSpec((block, v_w), lambda s, g, t: (s * nb + t, z_col0 // v_w + g)),
                pl.BlockSpec((1, n_chunks, chunk, LANES), lambda s, g, t: (g, s * nb + t, 0, 0)),
                pl.BlockSpec((1, n_chunks, GDN_HG, chunk), lambda s, g, t: (g, s * nb + t, 0, 0)),
                pl.BlockSpec((1, GDN_DV), lambda s, g, t: (0, 0))]
    args = [qkv, qkv, qkv, z, gb, gt, g_out.reshape(1, GDN_DV)]
    if has_s0:
        in_specs.append(pl.BlockSpec((1, GDN_HG, GDN_DK, GDN_DV), lambda s, g, t: (s, g, 0, 0)))
        args.append(s0)
    return pl.pallas_call(
        kern,
        grid=(n_seq, GDN_NG, nb),
        in_specs=in_specs,
        out_specs=[pl.BlockSpec((block, v_w), lambda s, g, t: (s * nb + t, g)),
                   pl.BlockSpec((1, GDN_HG, GDN_DK, GDN_DV), lambda s, g, t: (s, g, 0, 0))],
        out_shape=[jax.ShapeDtypeStruct((rows, GDN_VD), bf16),
                   jax.ShapeDtypeStruct((n_seq, GDN_HV, GDN_DK, GDN_DV), f32)],
        scratch_shapes=[pltpu.VMEM((GDN_HG, GDN_DK, GDN_DV), f32)],
        compiler_params=_params(("parallel", "parallel", "arbitrary"), VMEM_BIG),
        name="gdn_chunks",
    )(*args)


def _gate_tables(bg, n_seq, seq_len, chunk):
    n_ch = n_seq * seq_len // chunk
    beta = bg[:, :GDN_HV].reshape(n_ch, chunk, GDN_NG, GDN_HG)
    g = bg[:, GDN_HV:2 * GDN_HV].reshape(n_ch, chunk, GDN_NG, GDN_HG)
    gb = jnp.concatenate([g, beta, jnp.zeros((n_ch, chunk, GDN_NG, LANES - 2 * GDN_HG), f32)], -1)
    return gb.transpose(2, 0, 1, 3), g.transpose(2, 0, 3, 1)


def _pad_axis(a, axis, size):
    pad = [(0, 0)] * a.ndim
    pad[axis] = (0, size - a.shape[axis])
    return jnp.pad(a, pad)


def _mla_layer(x, j, cache_mla, pt_flat, w_in, g_q, g_kv, w_uq, w_uk, w_uv, w_out, ln_g, ln_b, tabs):
    w_pad = _pad_axis(w_in[j], 1, MLA_IN_PAD).astype(bf16)
    cq, rows = _mla_in(x, w_pad, g_q[j], g_kv[j], tabs)
    wq = w_uq[j].reshape(MLA_Q_LORA, MLA_HEADS, MLA_NOPE + MLA_ROPE)
    wn = wq[:, :, :MLA_NOPE].transpose(1, 0, 2).astype(bf16)
    wr = _pad_axis(wq[:, :, MLA_NOPE:], 2, LANES).transpose(1, 0, 2).astype(bf16)
    wuk = w_uk[j].transpose(1, 2, 0).astype(bf16)
    q_cat = _mla_q(cq, wn, wr, wuk, tabs)
    o_p = _mla_flash(q_cat, rows.astype(bf16))
    q_s = (q_cat[:, NP_TOK:].reshape(MLA_HEADS, DEC_BATCH, DEC_SEQ, MLA_CACHE)
           .transpose(1, 0, 2, 3).reshape(DEC_BATCH, MLA_SROWS, MLA_CACHE))
    rows_new = _pad_axis(rows[NP_TOK:].reshape(DEC_BATCH, DEC_SEQ, MLA_CACHE), 1, NEW_PAD)
    o_s = _mla_paged(pt_flat, q_s, rows_new, cache_mla.transpose(0, 1, 3, 2), j)
    o_s = (o_s.reshape(DEC_BATCH, MLA_HEADS, DEC_SEQ, MLA_KV_LORA)
           .transpose(1, 0, 2, 3).reshape(MLA_HEADS, NS_TOK, MLA_KV_LORA))
    v = _mla_uv(jnp.concatenate([o_p, o_s], 1), w_uv[j].transpose(1, 0, 2).astype(bf16))
    return _proj_ln(v, w_out[j].astype(bf16), x, ln_g, ln_b), rows


GDN_SPAD = 8


def _gdn_layer(x, j, state_s, state_conv, w_in, w_conv, a_log, dt_bias, g_out, w_out, ln_g, ln_b):
    w = w_in[j]
    hg = _matmul(x, w[:, :GDN_MAIN].astype(bf16), tm=512, tn=1024)
    bg = _gdn_ba(x, _pad_axis(w[:, GDN_MAIN:], 1, LANES).astype(bf16), a_log[j], dt_bias[j])
    wct = w_conv[j].T
    conv_p = _conv_prompt(hg, wct)
    mixed_s = hg[NP_TOK:, :GDN_CONV_DIM].reshape(DEC_BATCH, DEC_SEQ, GDN_CONV_DIM)
    conv_s = _conv_sample(mixed_s.transpose(1, 0, 2), state_conv[j].transpose(1, 0, 2), wct)
    conv_s = _pad_axis(conv_s.transpose(1, 0, 2), 1, GDN_SPAD).reshape(DEC_BATCH * GDN_SPAD, GDN_CONV_DIM)
    z_s = _pad_axis(hg[NP_TOK:, GDN_CONV_DIM:].reshape(DEC_BATCH, DEC_SEQ, GDN_VD), 1, GDN_SPAD)
    z_s = z_s.reshape(DEC_BATCH * GDN_SPAD, GDN_VD)
    bg_s = _pad_axis(bg[NP_TOK:].reshape(DEC_BATCH, DEC_SEQ, LANES), 1, GDN_SPAD).reshape(-1, LANES)
    gb_p, gt_p = _gate_tables(bg[:NP_TOK], BATCH, SEQ, GDN_CHUNK)
    gb_s, gt_s = _gate_tables(bg_s, DEC_BATCH, GDN_SPAD, GDN_SPAD)
    o_p, s_p = _gdn_chunks(conv_p, hg, GDN_CONV_DIM, gb_p, gt_p, g_out[j], None,
                           n_seq=BATCH, seq_len=SEQ, chunk=GDN_CHUNK, block=4 * GDN_CHUNK)
    o_s, s_s = _gdn_chunks(conv_s, z_s, 0, gb_s, gt_s, g_out[j], state_s[j],
                           n_seq=DEC_BATCH, seq_len=GDN_SPAD, chunk=GDN_SPAD, block=GDN_SPAD)
    o_s = o_s.reshape(DEC_BATCH, GDN_SPAD, GDN_VD)[:, :DEC_SEQ].reshape(NS_TOK, GDN_VD)
    x1 = _proj_ln(jnp.concatenate([o_p, o_s], 0), w_out[j].astype(bf16), x, ln_g, ln_b)
    buf_p = hg[:NP_TOK, :GDN_CONV_DIM].reshape(BATCH, SEQ, GDN_CONV_DIM)[:, SEQ - (GDN_CONV - 1):]
    buf_s = mixed_s[:, DEC_SEQ - (GDN_CONV - 1):]
    return x1, s_p, buf_p, s_s, buf_s


def _dsa_layer(x, j, cache_kv, cache_idx, pt_flat, w_in, w_out, ln_g, ln_b, tabs):
    w = w_in[j]
    off_k = DSA_HEADS * DSA_HD
    off_iq = off_k + 2 * DSA_KV_HEADS * DSA_HD
    off_ik = off_iq + DSA_IDX_HEADS * DSA_IDX_DIM
    off_iw = off_ik + DSA_IDX_DIM
    w_re = jnp.concatenate([w[:, off_iq:off_ik], w[:, :off_k], w[:, off_k:off_iq], w[:, off_ik:off_iw],
                            _pad_axis(w[:, off_iw:], 1, LANES)], 1).astype(bf16)
    hd = _dsa_in(x, w_re, tabs)
    attn_p = _dsa_prompt(hd)
    hs = hd[NP_TOK:]
    iq_s = hs[:, DSA_COL_IQ:DSA_COL_Q].reshape(DEC_BATCH, DEC_SEQ, DSA_IDX_HEADS, DSA_IDX_DIM)
    iq_s = jnp.concatenate([iq_s, iq_s], 1).reshape(DEC_BATCH, DSI_ROWS * DSA_IDX_HEADS, DSA_IDX_DIM).astype(bf16)
    iw_s = hs[:, DSA_COL_IW:DSA_COL_IW + DSA_IDX_HEADS].reshape(DEC_BATCH, DEC_SEQ, DSA_IDX_HEADS)
    iw_s = jnp.concatenate([iw_s, iw_s], 1).reshape(DEC_BATCH, DSI_ROWS * DSA_IDX_HEADS, 1)
    ik_new = _pad_axis(hs[:, DSA_COL_IK:DSA_COL_IW].reshape(DEC_BATCH, DEC_SEQ, DSA_IDX_DIM), 1, LANES)
    mask = _dsa_idx(pt_flat, iq_s, iw_s, ik_new, cache_idx, j)
    q_s = (hs[:, DSA_COL_Q:DSA_COL_KV].reshape(DEC_BATCH, DEC_SEQ, DSA_HEADS, DSA_HD)
           .transpose(0, 2, 1, 3).reshape(DEC_BATCH, DSP_ROWS, DSA_HD).astype(bf16))
    kv_new = _pad_axis(hs[:, DSA_COL_KV:DSA_COL_IK].reshape(DEC_BATCH, DEC_SEQ * KV_SLICES, DSA_HD), 1, PAGE_ROWS)
    cache_rows = cache_kv.reshape(cache_kv.shape[0], cache_kv.shape[1], PAGE_ROWS, DSA_HD)
    o_s = _dsa_paged(pt_flat, q_s, mask, kv_new, cache_rows, j)
    attn_s = (o_s.reshape(DEC_BATCH, DSA_HEADS, DEC_SEQ, DSA_HD)
              .transpose(0, 2, 1, 3).reshape(NS_TOK, DSA_HEADS * DSA_HD))
    x1 = _proj_ln(jnp.concatenate([attn_p, attn_s], 0), w_out[j].astype(bf16), x, ln_g, ln_b)
    kv = hd[:, DSA_COL_KV:DSA_COL_IK]
    ik = hd[:, DSA_COL_IK:DSA_COL_IW]
    return x1, kv, ik


def kernel(x_prompt, x_sample, cache_mla, state_gdn_S, state_gdn_conv, cache_dsa_kv, cache_dsa_idx, page_table,
           w_mla_in, g_mla_q, g_mla_kv, w_mla_uq, w_mla_uk, w_mla_uv, w_mla_out,
           w_gdn_in, w_gdn_conv, gdn_a_log, gdn_dt_bias, g_gdn_out, w_gdn_out, w_dsa_in, w_dsa_out,
           ln1_g, ln1_b, ln2_g, ln2_b, w_up, w_down):
    x = jnp.concatenate([x_prompt.reshape(NP_TOK, D_MODEL), x_sample.reshape(NS_TOK, D_MODEL)], 0)
    pos = jnp.concatenate([jnp.tile(jnp.arange(SEQ), BATCH), jnp.tile(PAST_LEN + jnp.arange(DEC_SEQ), DEC_BATCH)])
    tabs_mla = _rope_tables(pos, MLA_ROPE)
    tabs_dsa = _rope_tables(pos, DSA_ROT)
    pt_flat = page_table.reshape(-1).astype(jnp.int32)
    mla_rows, gdn_out, dsa_out = [], [], []
    for i in range(DEPTH):
        kind, j = i % N_MIXERS, i // N_MIXERS
        if kind == 0:
            x, rows = _mla_layer(x, j, cache_mla, pt_flat, w_mla_in, g_mla_q, g_mla_kv, w_mla_uq, w_mla_uk,
                                 w_mla_uv, w_mla_out, ln1_g[i], ln1_b[i], tabs_mla)
            mla_rows.append(rows)
        elif kind == 1:
            x, *st = _gdn_layer(x, j, state_gdn_S, state_gdn_conv, w_gdn_in, w_gdn_conv, gdn_a_log, gdn_dt_bias,
                                g_gdn_out, w_gdn_out, ln1_g[i], ln1_b[i])
            gdn_out.append(st)
        else:
            x, kv, ik = _dsa_layer(x, j, cache_dsa_kv, cache_dsa_idx, pt_flat, w_dsa_in, w_dsa_out,
                                   ln1_g[i], ln1_b[i], tabs_dsa)
            dsa_out.append((kv, ik))
        x = _mlp_ln(x, w_up[i].astype(bf16), w_down[i].astype(bf16), ln2_g[i], ln2_b[i])

    kv_shape = (2, DSA_KV_HEADS, DSA_HD)
    return (
        x[:NP_TOK].reshape(BATCH, SEQ, D_MODEL),
        x[NP_TOK:].reshape(DEC_BATCH, DEC_SEQ, D_MODEL),
        jnp.stack([r[:NP_TOK].reshape(BATCH, SEQ, MLA_CACHE) for r in mla_rows]),
        jnp.stack([r[NP_TOK:].reshape(DEC_BATCH, DEC_SEQ, MLA_CACHE) for r in mla_rows]),
        jnp.stack([st[0] for st in gdn_out]),
        jnp.stack([st[1] for st in gdn_out]),
        jnp.stack([st[2] for st in gdn_out]),
        jnp.stack([st[3] for st in gdn_out]),
        jnp.stack([kv[:NP_TOK].reshape((BATCH, SEQ) + kv_shape) for kv, _ in dsa_out]),
        jnp.stack([ik[:NP_TOK].reshape(BATCH, SEQ, DSA_IDX_DIM) for _, ik in dsa_out]),
        jnp.stack([kv[NP_TOK:].reshape((DEC_BATCH, DEC_SEQ) + kv_shape) for kv, _ in dsa_out]),
        jnp.stack([ik[NP_TOK:].reshape(DEC_BATCH, DEC_SEQ, DSA_IDX_DIM) for _, ik in dsa_out]),
    )
```
